```python
import math
import jax
import jax.numpy as jnp
from jax import lax
import numpy as np

D_MODEL = 1024
BATCH = 2
SEQ = 16384
DEPTH = 2

HEAD_DIM = 128
N_RET_HEADS = 4
N_GDN_HEADS = 4
N_MOBA_HEADS = D_MODEL // HEAD_DIM
D_RET = N_RET_HEADS * HEAD_DIM
D_GDN = N_GDN_HEADS * HEAD_DIM
D_IN_EVEN = 4 * D_RET + 4 * D_GDN + 2 * N_GDN_HEADS
LIN_CHUNK = 64
CONV_WIDTH = 4
MOBA_BLOCK = 256
MOBA_TOPK = 3
MOBA_Q_CHUNK = 64
D_FF_DENSE = 2816
N_EXPERTS = 8
TOP_K = 2
D_FF_EXPERT = 3584
EXPERT_ROW_BLOCK = 512
N_EVEN = (DEPTH + 1) // 2
N_ODD = DEPTH // 2
NORM_EPS = 1e-6
NEG_INF = -1e30

kernel_name = 'hybrid_retnet_gdn_moba_moe'


def rms_norm(x, gain):
    xf = x.astype(jnp.float32)
    y = xf * lax.rsqrt(jnp.mean(xf * xf, axis=-1, keepdims=True) + NORM_EPS)
    return (y * gain.astype(jnp.float32)).astype(x.dtype)


def l2_normalize(x):
    return x * lax.rsqrt(jnp.sum(x * x, axis=-1, keepdims=True) + NORM_EPS)


def to_heads(x, n_heads):
    b, l, _ = x.shape
    return x.reshape(b, l, n_heads, HEAD_DIM).transpose(0, 2, 1, 3)


def to_chunks(x):
    b, h, l = x.shape[:3]
    return x.reshape(b, h, l // LIN_CHUNK, LIN_CHUNK, *x.shape[3:])


def swiglu(x, w_gate, w_up, w_down):
    return (jax.nn.silu(x @ w_gate) * (x @ w_up)) @ w_down


def causal_depthwise_conv(x, w):
    ch = x.shape[-1]
    return lax.conv_general_dilated(
        x, w[:, None, :], window_strides=(1,), padding=((CONV_WIDTH - 1, 0),),
        dimension_numbers=('NWC', 'WIO', 'NWC'), feature_group_count=ch)


def retention_chunkwise(q, k, v):
    b, h, l, d = q.shape
    log_gamma = jnp.log1p(-jnp.exp2(-5.0 - jnp.arange(h, dtype=jnp.float32)))
    pos = jnp.arange(LIN_CHUNK, dtype=jnp.float32)
    diff = pos[:, None] - pos[None, :]
    d_mask = jnp.where(diff >= 0.0, jnp.exp(log_gamma[:, None, None] * jnp.maximum(diff, 0.0)), 0.0)
    qc, kc, vc = to_chunks(q), to_chunks(k * d ** -0.5), to_chunks(v)
    scores = jnp.einsum('bhnid,bhnjd->bhnij', qc, kc) * d_mask[None, :, None]
    o_intra = jnp.einsum('bhnij,bhnjd->bhnid', scores, vc)
    q_dec = qc * jnp.exp(log_gamma[:, None] * (pos + 1.0))[None, :, None, :, None]
    k_dec = kc * jnp.exp(log_gamma[:, None] * (LIN_CHUNK - 1.0 - pos))[None, :, None, :, None]
    chunk_decay = jnp.exp(log_gamma * LIN_CHUNK)[None, :, None, None]

    def step(state, xs):
        q_i, k_i, v_i = xs
        o_i = jnp.einsum('bhcd,bhde->bhce', q_i, state)
        state = state * chunk_decay + jnp.einsum('bhcd,bhce->bhde', k_i, v_i)
        return state, o_i

    state0 = jnp.zeros((b, h, d, d), q.dtype)
    _, o_inter = lax.scan(step, state0, (jnp.moveaxis(q_dec, 2, 0), jnp.moveaxis(k_dec, 2, 0), jnp.moveaxis(vc, 2, 0)))
    o = o_intra + jnp.moveaxis(o_inter, 0, 2)
    return o.reshape(b, h, l, d)


def gated_delta_rule_chunked(q, k, v, g, beta):
    b, h, l, d = q.shape
    idx = jnp.arange(LIN_CHUNK)
    incl = idx[:, None] >= idx[None, :]
    strict = idx[:, None] > idx[None, :]
    eye = jnp.eye(LIN_CHUNK, dtype=q.dtype)
    qc, kc, vc = to_chunks(q * d ** -0.5), to_chunks(k), to_chunks(v)
    gcum = jnp.cumsum(to_chunks(g), axis=-1)
    bc = to_chunks(beta)[..., None]
    gdiff = gcum[..., :, None] - gcum[..., None, :]
    decay = jnp.where(incl, jnp.exp(jnp.where(incl, gdiff, 0.0)), 0.0)
    k_beta = kc * bc
    lower = jnp.where(strict, jnp.einsum('bhnid,bhnjd->bhnij', k_beta, kc) * decay, 0.0)
    t_mat = lax.linalg.triangular_solve(eye + lower, jnp.broadcast_to(eye, lower.shape), left_side=True, lower=True)
    u = t_mat @ (vc * bc)
    w = t_mat @ (k_beta * jnp.exp(gcum)[..., None])
    attn = jnp.where(incl, jnp.einsum('bhnid,bhnjd->bhnij', qc, kc) * decay, 0.0)
    q_dec = qc * jnp.exp(gcum)[..., None]
    k_dec = kc * jnp.exp(gcum[..., -1:] - gcum)[..., None]
    chunk_decay = jnp.exp(gcum[..., -1])

    def step(state, xs):
        w_i, u_i, q_i, k_i, a_i, dec_i = xs
        v_new = u_i - w_i @ state
        o_i = q_i @ state + a_i @ v_new
        state = state * dec_i[..., None, None] + jnp.einsum('bhcd,bhce->bhde', k_i, v_new)
        return state, o_i

    xs = (jnp.moveaxis(w, 2, 0), jnp.moveaxis(u, 2, 0), jnp.moveaxis(q_dec, 2, 0),
          jnp.moveaxis(k_dec, 2, 0), jnp.moveaxis(attn, 2, 0), jnp.moveaxis(chunk_decay, 2, 0))
    state0 = jnp.zeros((b, h, d, d), q.dtype)
    _, o = lax.scan(step, state0, xs)
    return jnp.moveaxis(o, 0, 2).reshape(b, h, l, d)


def retention_deltanet_mixer(h, w_in, conv_w, a_log, dt_bias, ret_norm, gdn_norm, w_out):
    b, l, _ = h.shape
    proj = (h @ w_in).astype(jnp.float32)
    cuts = [D_RET, 2 * D_RET, 3 * D_RET, 4 * D_RET,
            4 * D_RET + 3 * D_GDN, 4 * D_RET + 4 * D_GDN, 4 * D_RET + 4 * D_GDN + N_GDN_HEADS]
    q_a, k_a, v_a, g_a, qkv_b, g_b, beta_raw, a_raw = jnp.split(proj, cuts, axis=-1)
    o_a = retention_chunkwise(to_heads(q_a, N_RET_HEADS), to_heads(k_a, N_RET_HEADS), to_heads(v_a, N_RET_HEADS))
    o_a = rms_norm(o_a.transpose(0, 2, 1, 3), ret_norm).reshape(b, l, D_RET) * jax.nn.silu(g_a)
    qkv_b = jax.nn.silu(causal_depthwise_conv(qkv_b, conv_w.astype(jnp.float32)))
    q_b, k_b, v_b = jnp.split(qkv_b, 3, axis=-1)
    q_b = l2_normalize(to_heads(q_b, N_GDN_HEADS))
    k_b = l2_normalize(to_heads(k_b, N_GDN_HEADS))
    v_b = to_heads(v_b, N_GDN_HEADS)
    beta = jax.nn.sigmoid(beta_raw).transpose(0, 2, 1)
    g = (-jnp.exp(a_log.astype(jnp.float32)) * jax.nn.softplus(a_raw + dt_bias.astype(jnp.float32))).transpose(0, 2, 1)
    o_b = gated_delta_rule_chunked(q_b, k_b, v_b, g, beta)
    o_b = rms_norm(o_b.transpose(0, 2, 1, 3), gdn_norm).reshape(b, l, D_GDN) * jax.nn.silu(g_b)
    mixed = jnp.concatenate([o_a, o_b], axis=-1).astype(h.dtype)
    return mixed @ w_out


def moba_mixer(h, w_qkv, w_out):
    b, l, _ = h.shape
    n_h = N_MOBA_HEADS
    d = HEAD_DIM
    q, k, v = jnp.split((h @ w_qkv).astype(jnp.float32), 3, axis=-1)
    q, k, v = to_heads(q, n_h), to_heads(k, n_h), to_heads(v, n_h)
    n_blk = -(-l // MOBA_BLOCK)
    pad = n_blk * MOBA_BLOCK - l
    kb = jnp.pad(k, ((0, 0), (0, 0), (0, pad), (0, 0))).reshape(b, n_h, n_blk, MOBA_BLOCK, d)
    vb = jnp.pad(v, ((0, 0), (0, 0), (0, pad), (0, 0))).reshape(b, n_h, n_blk, MOBA_BLOCK, d)
    k_mean = jnp.mean(kb, axis=3)
    slopes = jnp.exp2(-8.0 * jnp.arange(1, n_h + 1, dtype=jnp.float32) / n_h)
    n_sel = min(MOBA_TOPK, n_blk)
    n_qc = l // MOBA_Q_CHUNK
    q_chunks = jnp.moveaxis(q.reshape(b, n_h, n_qc, MOBA_Q_CHUNK, d), 2, 0)
    scale = d ** -0.5
    blk_ids = jnp.arange(n_blk)
    offs = jnp.arange(MOBA_BLOCK)
    bi = jnp.arange(b)[:, None, None, None]
    hi = jnp.arange(n_h)[None, :, None, None]

    def attend(args):
        c, q_c = args
        t = c * MOBA_Q_CHUNK + jnp.arange(MOBA_Q_CHUNK)
        own = (c * MOBA_Q_CHUNK) // MOBA_BLOCK
        gate = jnp.einsum('bhqd,bhnd->bhqn', q_c, k_mean)
        gate = jnp.where(blk_ids < own, gate, NEG_INF)
        _, sel = lax.top_k(gate, n_sel)
        valid = sel < own
        k_sel = kb[bi, hi, sel]
        pos_sel = sel[..., None] * MOBA_BLOCK + offs
        dist_sel = (t[:, None, None] - pos_sel).astype(jnp.float32)
        s_sel = jnp.einsum('bhqd,bhqjsd->bhqjs', q_c, k_sel) * scale - slopes[:, None, None, None] * dist_sel
        s_sel = jnp.where(valid[..., None], s_sel, NEG_INF)
        k_own = lax.dynamic_index_in_dim(kb, own, axis=2, keepdims=False)
        v_own = lax.dynamic_index_in_dim(vb, own, axis=2, keepdims=False)
        pos_own = own * MOBA_BLOCK + offs
        dist_own = (t[:, None] - pos_own[None, :]).astype(jnp.float32)
        s_own = jnp.einsum('bhqd,bhsd->bhqs', q_c, k_own) * scale - slopes[:, None, None] * dist_own
        s_own = jnp.where(dist_own >= 0.0, s_own, NEG_INF)
        probs = jax.nn.softmax(jnp.concatenate(
            [s_own, s_sel.reshape(b, n_h, MOBA_Q_CHUNK, n_sel * MOBA_BLOCK)], axis=-1), axis=-1)
        p_own = probs[..., :MOBA_BLOCK]
        p_sel = probs[..., MOBA_BLOCK:].reshape(b, n_h, MOBA_Q_CHUNK, n_sel, MOBA_BLOCK)
        v_sel = vb[bi, hi, sel]
        return jnp.einsum('bhqs,bhsd->bhqd', p_own, v_own) + jnp.einsum('bhqjs,bhqjsd->bhqd', p_sel, v_sel)

    o = lax.map(attend, (jnp.arange(n_qc), q_chunks))
    o = o.transpose(1, 0, 3, 2, 4).reshape(b, l, n_h * d)
    return o.astype(h.dtype) @ w_out


def moe_swiglu(h, w_router, w_gate, w_up, w_down):
    b, l, dm = h.shape
    xt = h.reshape(b * l, dm)
    n_assign = b * l * TOP_K
    logits = (xt @ w_router).astype(jnp.float32)
    top_logit, top_idx = lax.top_k(logits, TOP_K)
    gate = jax.nn.softmax(top_logit, axis=-1)
    expert_of = top_idx.reshape(-1)
    order = jnp.argsort(expert_of)
    sorted_expert = expert_of[order]
    counts = jnp.bincount(expert_of, length=N_EXPERTS)
    padded = (counts + EXPERT_ROW_BLOCK - 1) // EXPERT_ROW_BLOCK * EXPERT_ROW_BLOCK
    padded_end = jnp.cumsum(padded)
    padded_start = padded_end - padded
    start = jnp.cumsum(counts) - counts
    dest = padded_start[sorted_expert] + jnp.arange(n_assign) - start[sorted_expert]
    n_blocks = -(-n_assign // EXPERT_ROW_BLOCK) + N_EXPERTS
    buf = jnp.zeros((n_blocks * EXPERT_ROW_BLOCK, dm), h.dtype).at[dest].set(xt[order // TOP_K])
    block_expert = jnp.minimum(
        jnp.searchsorted(padded_end, jnp.arange(n_blocks) * EXPERT_ROW_BLOCK, side='right'), N_EXPERTS - 1)

    def run_block(args):
        e, rows = args
        return swiglu(rows, w_gate[e], w_up[e], w_down[e])

    y = lax.map(run_block, (block_expert, buf.reshape(n_blocks, EXPERT_ROW_BLOCK, dm)))
    y_sorted = y.reshape(-1, dm)[dest]
    y_assign = jnp.zeros((n_assign, dm), h.dtype).at[order].set(y_sorted).reshape(b * l, TOP_K, dm)
    out = jnp.einsum('tk,tkd->td', gate.astype(h.dtype), y_assign)
    return out.reshape(b, l, dm)


def setup_inputs(seed: int = 0) -> dict:
    key = jax.random.key(seed)
    ks = jax.random.split(key, 20)
    f32 = jnp.float32

    def dense(k, shape, fan_in):
        return jax.random.normal(k, shape, f32) * fan_in ** -0.5

    def gain(k, shape):
        return 1.0 + 0.02 * jax.random.normal(k, shape, f32)

    x = jax.random.normal(ks[0], (BATCH, SEQ, D_MODEL), f32)
    mix_norm = gain(ks[1], (DEPTH, D_MODEL))
    ffn_norm = gain(ks[2], (DEPTH, D_MODEL))
    even_w_in = dense(ks[3], (N_EVEN, D_MODEL, D_IN_EVEN), D_MODEL)
    even_conv_w = dense(ks[4], (N_EVEN, CONV_WIDTH, 3 * D_GDN), CONV_WIDTH)
    even_a_log = jnp.log(jax.random.uniform(ks[5], (N_EVEN, N_GDN_HEADS), f32, 1.0, 16.0))
    dt = jnp.exp(jax.random.uniform(ks[6], (N_EVEN, N_GDN_HEADS), f32, math.log(1e-3), math.log(1e-1)))
    even_dt_bias = dt + jnp.log(-jnp.expm1(-dt))
    even_ret_norm = gain(ks[7], (N_EVEN, N_RET_HEADS, HEAD_DIM))
    even_gdn_norm = gain(ks[8], (N_EVEN, N_GDN_HEADS, HEAD_DIM))
    even_w_out = dense(ks[9], (N_EVEN, D_RET + D_GDN, D_MODEL), D_RET + D_GDN)
    odd_w_qkv = dense(ks[10], (N_ODD, D_MODEL, 3 * D_MODEL), D_MODEL)
    odd_w_out = dense(ks[11], (N_ODD, D_MODEL, D_MODEL), D_MODEL)
    ffn_w_gate = dense(ks[12], (N_EVEN, D_MODEL, D_FF_DENSE), D_MODEL)
    ffn_w_up = dense(ks[13], (N_EVEN, D_MODEL, D_FF_DENSE), D_MODEL)
    ffn_w_down = dense(ks[14], (N_EVEN, D_FF_DENSE, D_MODEL), D_FF_DENSE)
    moe_router = dense(ks[15], (N_ODD, D_MODEL, N_EXPERTS), D_MODEL)
    moe_w_gate = dense(ks[16], (N_ODD, N_EXPERTS, D_MODEL, D_FF_EXPERT), D_MODEL)
    moe_w_up = dense(ks[17], (N_ODD, N_EXPERTS, D_MODEL, D_FF_EXPERT), D_MODEL)
    moe_w_down = dense(ks[18], (N_ODD, N_EXPERTS, D_FF_EXPERT, D_MODEL), D_FF_EXPERT)
    final_norm = gain(ks[19], (D_MODEL,))
    return {'x': x, 'mix_norm': mix_norm, 'ffn_norm': ffn_norm,
            'even_w_in': even_w_in, 'even_conv_w': even_conv_w, 'even_a_log': even_a_log,
            'even_dt_bias': even_dt_bias, 'even_ret_norm': even_ret_norm, 'even_gdn_norm': even_gdn_norm,
            'even_w_out': even_w_out, 'odd_w_qkv': odd_w_qkv, 'odd_w_out': odd_w_out,
            'ffn_w_gate': ffn_w_gate, 'ffn_w_up': ffn_w_up, 'ffn_w_down': ffn_w_down,
            'moe_router': moe_router, 'moe_w_gate': moe_w_gate, 'moe_w_up': moe_w_up,
            'moe_w_down': moe_w_down, 'final_norm': final_norm}


def reference(x, mix_norm, ffn_norm, even_w_in, even_conv_w, even_a_log, even_dt_bias,
              even_ret_norm, even_gdn_norm, even_w_out, odd_w_qkv, odd_w_out,
              ffn_w_gate, ffn_w_up, ffn_w_down, moe_router, moe_w_gate, moe_w_up,
              moe_w_down, final_norm):
    h = x
    for layer in range(DEPTH):
        i = layer // 2
        hn = rms_norm(h, mix_norm[layer])
        if layer % 2 == 0:
            h = h + retention_deltanet_mixer(hn, even_w_in[i], even_conv_w[i], even_a_log[i], even_dt_bias[i],
                                             even_ret_norm[i], even_gdn_norm[i], even_w_out[i])
            h = h + swiglu(rms_norm(h, ffn_norm[layer]), ffn_w_gate[i], ffn_w_up[i], ffn_w_down[i])
        else:
            h = h + moba_mixer(hn, odd_w_qkv[i], odd_w_out[i])
            h = h + moe_swiglu(rms_norm(h, ffn_norm[layer]), moe_router[i], moe_w_gate[i], moe_w_up[i], moe_w_down[i])
    return rms_norm(h, final_norm)
```

```python
import functools
import math

import jax
import jax.numpy as jnp
from jax import lax
from jax.experimental import pallas as pl
from jax.experimental.pallas import tpu as pltpu

F32 = jnp.float32
BF16 = jnp.bfloat16

HEAD_DIM = 128
N_RET_HEADS = 4
N_GDN_HEADS = 4
D_RET = N_RET_HEADS * HEAD_DIM
D_GDN = N_GDN_HEADS * HEAD_DIM
CONV_WIDTH = 4
MOBA_BLOCK = 256
MOBA_TOPK = 3
N_EXPERTS = 8
TOP_K = 2
EXPERT_ROW_BLOCK = 512
NORM_EPS = 1e-6
NEG_INF = -1e30

LANES = 128
SEQ_CHUNK = 256
VMEM_LIMIT = 56 * 1024 * 1024


def _cparams(*sem):
    return pltpu.CompilerParams(dimension_semantics=sem, vmem_limit_bytes=VMEM_LIMIT)


def _dot(a, b):
    return jnp.dot(a, b, preferred_element_type=F32)


def _dot_nt(a, b):
    return lax.dot_general(a, b, (((1,), (1,)), ((), ())), preferred_element_type=F32)


def _dot_tn(a, b):
    return lax.dot_general(a, b, (((0,), (0,)), ((), ())), preferred_element_type=F32)


def _rms(x, gain):
    return x * lax.rsqrt(jnp.mean(x * x, axis=-1, keepdims=True) + NORM_EPS) * gain


def _silu(x):
    return x / (1.0 + jnp.exp(-x))


def _rmsnorm_kernel(x_ref, g_ref, o_ref):
    o_ref[...] = _rms(x_ref[...], g_ref[...]).astype(o_ref.dtype)


def rmsnorm(x, gain, out_dtype, tm=1024):
    t, d = x.shape
    return pl.pallas_call(
        _rmsnorm_kernel,
        grid=(t // tm,),
        in_specs=[pl.BlockSpec((tm, d), lambda i: (i, 0)),
                  pl.BlockSpec((1, d), lambda i: (0, 0))],
        out_specs=pl.BlockSpec((tm, d), lambda i: (i, 0)),
        out_shape=jax.ShapeDtypeStruct((t, d), out_dtype),
        compiler_params=_cparams("parallel"),
        name="rmsnorm",
    )(x, gain.reshape(1, d))


def _matmul_kernel(a_ref, w_ref, o_ref):
    o_ref[...] = _dot(a_ref[...], w_ref[...]).astype(o_ref.dtype)


def matmul(a, w, out_dtype, tm=1024, tn=1024):
    t, k = a.shape
    n = w.shape[1]
    tn = min(tn, n)
    return pl.pallas_call(
        _matmul_kernel,
        grid=(t // tm, n // tn),
        in_specs=[pl.BlockSpec((tm, k), lambda i, j: (i, 0)),
                  pl.BlockSpec((k, tn), lambda i, j: (0, j))],
        out_specs=pl.BlockSpec((tm, tn), lambda i, j: (i, j)),
        out_shape=jax.ShapeDtypeStruct((t, n), out_dtype),
        compiler_params=_cparams("parallel", "parallel"),
        name="matmul",
    )(a, w)


def _matmul_res_norm_kernel(a_ref, w_ref, r_ref, g_ref, h_ref, hn_ref):
    h = r_ref[...] + _dot(a_ref[...], w_ref[...])
    h_ref[...] = h
    hn_ref[...] = _rms(h, g_ref[...]).astype(hn_ref.dtype)


def matmul_res_norm(a, w, resid, gain, tm=512):
    t, k = a.shape
    d = w.shape[1]
    return pl.pallas_call(
        _matmul_res_norm_kernel,
        grid=(t // tm,),
        in_specs=[pl.BlockSpec((tm, k), lambda i: (i, 0)),
                  pl.BlockSpec((k, d), lambda i: (0, 0)),
                  pl.BlockSpec((tm, d), lambda i: (i, 0)),
                  pl.BlockSpec((1, d), lambda i: (0, 0))],
        out_specs=[pl.BlockSpec((tm, d), lambda i: (i, 0)),
                   pl.BlockSpec((tm, d), lambda i: (i, 0))],
        out_shape=[jax.ShapeDtypeStruct((t, d), F32),
                   jax.ShapeDtypeStruct((t, d), BF16)],
        compiler_params=_cparams("parallel"),
        name="matmul_res_norm",
    )(a, w, resid, gain.reshape(1, d))


def _ffn_kernel(x_ref, r_ref, wg_ref, wu_ref, wd_ref, g_ref, h_ref, hn_ref, acc_ref):
    j = pl.program_id(1)

    @pl.when(j == 0)
    def _():
        acc_ref[...] = r_ref[...]

    x = x_ref[...]
    act = (_silu(_dot(x, wg_ref[...])) * _dot(x, wu_ref[...])).astype(BF16)
    acc_ref[...] += _dot(act, wd_ref[...])

    @pl.when(j == pl.num_programs(1) - 1)
    def _():
        h = acc_ref[...]
        h_ref[...] = h
        hn_ref[...] = _rms(h, g_ref[...]).astype(hn_ref.dtype)


def ffn_res_norm(x, resid, wg, wu, wd, gain, tm=512, n_ff_tiles=2):
    t, d = x.shape
    ff = wg.shape[1]
    tf = ff // n_ff_tiles
    return pl.pallas_call(
        _ffn_kernel,
        grid=(t // tm, n_ff_tiles),
        in_specs=[pl.BlockSpec((tm, d), lambda i, j: (i, 0)),
                  pl.BlockSpec((tm, d), lambda i, j: (i, 0)),
                  pl.BlockSpec((d, tf), lambda i, j: (0, j)),
                  pl.BlockSpec((d, tf), lambda i, j: (0, j)),
                  pl.BlockSpec((tf, d), lambda i, j: (j, 0)),
                  pl.BlockSpec((1, d), lambda i, j: (0, 0))],
        out_specs=[pl.BlockSpec((tm, d), lambda i, j: (i, 0)),
                   pl.BlockSpec((tm, d), lambda i, j: (i, 0))],
        out_shape=[jax.ShapeDtypeStruct((t, d), F32),
                   jax.ShapeDtypeStruct((t, d), BF16)],
        scratch_shapes=[pltpu.VMEM((tm, d), F32)],
        compiler_params=_cparams("parallel", "arbitrary"),
        name="ffn_res_norm",
    )(x, resid, wg, wu, wd, gain.reshape(1, d))


def _mixer_kernel(p_ref, s_ref, cw_ref, alog_ref, dtb_ref, rn_ref, gn_ref, o_ref,
                  ext_ref, ret_state, gdn_state, lvl_ref):
    c = SEQ_CHUNK
    d = HEAD_DIM
    step = pl.program_id(1)
    row = lax.broadcasted_iota(jnp.int32, (c, c), 0)
    col = lax.broadcasted_iota(jnp.int32, (c, c), 1)
    diff = (row - col).astype(F32)
    incl = row >= col
    strict = row > col
    pos = lax.broadcasted_iota(jnp.int32, (c, 1), 0).astype(F32)

    @pl.when(step == 0)
    def _():
        ret_state[...] = jnp.zeros_like(ret_state)
        gdn_state[...] = jnp.zeros_like(gdn_state)
        ext_ref[0:8, :] = jnp.zeros((8, ext_ref.shape[1]), F32)
        for k in range(8):
            rb = row >> k
            cb = col >> k
            lvl_ref[k] = jnp.where((rb - cb == 1) & ((cb & 1) == 0), 1.0, 0.0)

    for h in range(N_RET_HEADS):
        lg = math.log1p(-2.0 ** (-5.0 - h))
        q = p_ref[:, h * d:(h + 1) * d]
        k = p_ref[:, D_RET + h * d:D_RET + (h + 1) * d] * d ** -0.5
        v = p_ref[:, 2 * D_RET + h * d:2 * D_RET + (h + 1) * d]
        gate = p_ref[:, 3 * D_RET + h * d:3 * D_RET + (h + 1) * d]
        d_mask = jnp.where(incl, jnp.exp(lg * jnp.maximum(diff, 0.0)), 0.0)
        vb = v.astype(BF16)
        scores = _dot_nt(q.astype(BF16), k.astype(BF16)) * d_mask
        state = ret_state[h]
        q_dec = q * jnp.exp(lg * (pos + 1.0))
        o = _dot(scores.astype(BF16), vb) + _dot(q_dec.astype(BF16), state.astype(BF16))
        k_dec = k * jnp.exp(lg * (c - 1.0 - pos))
        ret_state[h] = state * math.exp(lg * c) + _dot_tn(k_dec.astype(BF16), vb)
        o = _rms(o, rn_ref[h:h + 1, :]) * _silu(gate)
        o_ref[:, h * d:(h + 1) * d] = o.astype(o_ref.dtype)

    base = 4 * D_RET
    ext_ref[8:8 + c, :] = p_ref[:, base:base + 3 * D_GDN]

    def conv_silu(lo):
        acc = cw_ref[0:1, lo:lo + d] * ext_ref[5:5 + c, lo:lo + d]
        for t in range(1, CONV_WIDTH):
            acc += cw_ref[t:t + 1, lo:lo + d] * ext_ref[5 + t:5 + t + c, lo:lo + d]
        return _silu(acc)

    def l2n(x):
        return x * lax.rsqrt(jnp.sum(x * x, axis=-1, keepdims=True) + NORM_EPS)

    small = s_ref[...]
    tri_incl = jnp.where(incl, 1.0, 0.0)
    for h in range(N_GDN_HEADS):
        q = l2n(conv_silu(h * d)) * d ** -0.5
        k = l2n(conv_silu(D_GDN + h * d))
        v = conv_silu(2 * D_GDN + h * d)
        gate = p_ref[:, base + 3 * D_GDN + h * d:base + 3 * D_GDN + (h + 1) * d]
        beta = 1.0 / (1.0 + jnp.exp(-small[:, h:h + 1]))
        a_raw = small[:, N_GDN_HEADS + h:N_GDN_HEADS + h + 1] + dtb_ref[h]
        softplus = jnp.maximum(a_raw, 0.0) + jnp.log1p(jnp.exp(-jnp.abs(a_raw)))
        g = -jnp.exp(jnp.full((1, 1), alog_ref[h], F32)) * softplus
        gmat = jnp.where(strict, jnp.broadcast_to(g, (c, c)), 0.0)
        gdiff = lax.dot_general(tri_incl, gmat, (((1,), (0,)), ((), ())),
                                precision=lax.Precision.HIGHEST, preferred_element_type=F32)
        gcum = gdiff[:, 0:1] + g[0:1, 0:1]
        decay = jnp.where(incl, jnp.exp(jnp.where(incl, gdiff, 0.0)), 0.0)
        kb = k.astype(BF16)
        k_beta = k * beta
        lower = jnp.where(strict, _dot_nt(k_beta.astype(BF16), kb) * decay, 0.0)
        t_mat = jnp.where(row == col, 1.0, 0.0)
        for lv in range(8):
            off = (lower * lvl_ref[lv]).astype(BF16)
            tb = t_mat.astype(BF16)
            t_mat = t_mat - _dot(tb, _dot(off, tb).astype(BF16))
        e_g = jnp.exp(gcum)
        rhs = jnp.concatenate([v * beta, k_beta * e_g], axis=1).astype(BF16)
        uw = _dot(t_mat.astype(BF16), rhs)
        u = uw[:, :d]
        w = uw[:, d:]
        attn = jnp.where(incl, _dot_nt(q.astype(BF16), kb) * decay, 0.0)
        state = gdn_state[h]
        sb = state.astype(BF16)
        g_last = gcum[c - 1:c, :]
        v_new = u - _dot(w.astype(BF16), sb)
        vnb = v_new.astype(BF16)
        o = _dot((q * e_g).astype(BF16), sb) + _dot(attn.astype(BF16), vnb)
        k_dec = k * jnp.exp(g_last - gcum)
        gdn_state[h] = state * jnp.exp(g_last) + _dot_tn(k_dec.astype(BF16), vnb)
        o = _rms(o, gn_ref[h:h + 1, :]) * _silu(gate)
        o_ref[:, D_RET + h * d:D_RET + (h + 1) * d] = o.astype(o_ref.dtype)

    ext_ref[0:8, :] = ext_ref[c:c + 8, :]


def ret_gdn_mixer(proj, small, conv_w, a_log, dt_bias, ret_norm, gdn_norm):
    b, l, width = proj.shape
    c = SEQ_CHUNK
    smem = pl.BlockSpec(memory_space=pltpu.SMEM)
    return pl.pallas_call(
        _mixer_kernel,
        grid=(b, l // c),
        in_specs=[pl.BlockSpec((None, c, width), lambda i, j: (i, j, 0)),
                  pl.BlockSpec((None, c, LANES), lambda i, j: (i, j, 0)),
                  pl.BlockSpec((CONV_WIDTH, 3 * D_GDN), lambda i, j: (0, 0)),
                  smem, smem,
                  pl.BlockSpec((N_RET_HEADS, HEAD_DIM), lambda i, j: (0, 0)),
                  pl.BlockSpec((N_GDN_HEADS, HEAD_DIM), lambda i, j: (0, 0))],
        out_specs=pl.BlockSpec((None, c, D_RET + D_GDN), lambda i, j: (i, j, 0)),
        out_shape=jax.ShapeDtypeStruct((b, l, D_RET + D_GDN), BF16),
        scratch_shapes=[pltpu.VMEM((c + 8, 3 * D_GDN), F32),
                        pltpu.VMEM((N_RET_HEADS, HEAD_DIM, HEAD_DIM), F32),
                        pltpu.VMEM((N_GDN_HEADS, HEAD_DIM, HEAD_DIM), F32),
                        pltpu.VMEM((8, c, c), F32)],
        compiler_params=_cparams("arbitrary", "arbitrary"),
        name="ret_gdn_mixer",
    )(proj, small, conv_w, a_log, dt_bias, ret_norm, gdn_norm)


def _moba_kernel(slope_ref, q_ref, k_ref, v_ref, o_ref, kmean_ref):
    blk = MOBA_BLOCK
    n_blk = kmean_ref.shape[0]
    h = pl.program_id(1)
    i = pl.program_id(2)
    slope = slope_ref[h]
    scale = HEAD_DIM ** -0.5

    @pl.when(i == 0)
    def _():
        def mean_body(j, carry):
            rows = k_ref[pl.ds(pl.multiple_of(j * blk, blk), blk), :].astype(F32)
            kmean_ref[pl.ds(j, 1), :] = jnp.mean(rows, axis=0, keepdims=True)
            return carry
        lax.fori_loop(0, n_blk, mean_body, 0)

    q = q_ref[...]
    rel = (lax.broadcasted_iota(jnp.int32, (blk, blk), 0)
           - lax.broadcasted_iota(jnp.int32, (blk, blk), 1)).astype(F32)

    col = lax.broadcasted_iota(jnp.int32, (blk, n_blk), 1)
    gate = lax.dot_general(q.astype(F32), kmean_ref[...], (((1,), (1,)), ((), ())),
                           precision=lax.Precision.HIGHEST, preferred_element_type=F32)
    gate = jnp.where(col < i, gate, NEG_INF)
    sel = jnp.zeros((blk, n_blk), F32)
    for _ in range(min(MOBA_TOPK, n_blk)):
        best = jnp.max(gate, axis=1, keepdims=True)
        first = jnp.min(jnp.where(gate == best, col, n_blk), axis=1, keepdims=True)
        pick = col == first
        sel = jnp.where(pick & (col < i), 1.0, sel)
        gate = jnp.where(pick, -jnp.inf, gate)

    start = pl.multiple_of(i * blk, blk)
    s = _dot_nt(q, k_ref[pl.ds(start, blk), :]) * scale - slope * rel
    s = jnp.where(rel >= 0.0, s, NEG_INF)
    m0 = jnp.max(s, axis=1, keepdims=True)
    p = jnp.exp(s - m0)
    l0 = jnp.sum(p, axis=1, keepdims=True)
    acc0 = _dot(p.astype(BF16), v_ref[pl.ds(start, blk), :])

    def body(j, carry):
        m, l, acc = carry
        lo = pl.multiple_of(j * blk, blk)
        dist = rel + ((i - j) * blk).astype(F32)
        s = _dot_nt(q, k_ref[pl.ds(lo, blk), :]) * scale - slope * dist
        chosen = jnp.sum(jnp.where(col == j, sel, 0.0), axis=1, keepdims=True)
        s = jnp.where(chosen > 0.0, s, NEG_INF)
        m_new = jnp.maximum(m, jnp.max(s, axis=1, keepdims=True))
        alpha = jnp.exp(m - m_new)
        p = jnp.exp(s - m_new)
        l = alpha * l + jnp.sum(p, axis=1, keepdims=True)
        acc = alpha * acc + _dot(p.astype(BF16), v_ref[pl.ds(lo, blk), :])
        return m_new, l, acc

    _, l, acc = lax.fori_loop(0, i, body, (m0, l0, acc0))
    o_ref[...] = (acc / l).astype(o_ref.dtype)


def moba_attention(qkv, slopes, n_heads):
    b, l, _ = qkv.shape
    d = HEAD_DIM
    blk = MOBA_BLOCK
    assert l % blk == 0
    n_blk = l // blk
    return pl.pallas_call(
        _moba_kernel,
        grid=(b, n_heads, n_blk),
        in_specs=[pl.BlockSpec(memory_space=pltpu.SMEM),
                  pl.BlockSpec((None, blk, d), lambda bi, h, i: (bi, i, h)),
                  pl.BlockSpec((None, l, d), lambda bi, h, i: (bi, 0, n_heads + h)),
                  pl.BlockSpec((None, l, d), lambda bi, h, i: (bi, 0, 2 * n_heads + h))],
        out_specs=pl.BlockSpec((None, blk, d), lambda bi, h, i: (bi, i, h)),
        out_shape=jax.ShapeDtypeStruct((b, l, n_heads * d), BF16),
        scratch_shapes=[pltpu.VMEM((n_blk, d), F32)],
        compiler_params=_cparams("arbitrary", "arbitrary", "arbitrary"),
        name="moba_attention",
    )(slopes, qkv, qkv, qkv)


def _router_kernel(a_ref, w_ref, r_ref, g_ref, wr_ref, h_ref, hn_ref, meta_ref, gw_ref,
                   cnt_ref, carry_ref):
    tm = a_ref.shape[0]

    @pl.when(pl.program_id(0) == 0)
    def _():
        carry_ref[...] = jnp.zeros_like(carry_ref)

    h = r_ref[...] + _dot(a_ref[...], w_ref[...])
    h_ref[...] = h
    hn = _rms(h, g_ref[...])
    hn_ref[...] = hn
    logits = lax.dot_general(hn, wr_ref[...], (((1,), (0,)), ((), ())),
                             precision=lax.Precision.HIGHEST, preferred_element_type=F32)
    col = lax.broadcasted_iota(jnp.int32, (tm, LANES), 1)
    logits = jnp.where(col < N_EXPERTS, logits, -jnp.inf)
    m1 = jnp.max(logits, axis=1, keepdims=True)
    i1 = jnp.min(jnp.where(logits == m1, col, LANES), axis=1, keepdims=True)
    rest = jnp.where(col == i1, -jnp.inf, logits)
    m2 = jnp.max(rest, axis=1, keepdims=True)
    i2 = jnp.min(jnp.where(rest == m2, col, LANES), axis=1, keepdims=True)
    e = jnp.exp(m2 - m1)
    g1 = 1.0 / (1.0 + e)
    g2 = e / (1.0 + e)
    pick1 = col == i1
    pick2 = col == i2
    onehot = jnp.where(pick1 | pick2, 1.0, 0.0)
    tri = jnp.where(lax.broadcasted_iota(jnp.int32, (tm, tm), 0)
                    > lax.broadcasted_iota(jnp.int32, (tm, tm), 1), 1.0, 0.0).astype(BF16)
    rank = _dot(tri, onehot.astype(BF16)) + carry_ref[0:1, :]
    carry = carry_ref[0:1, :] + jnp.sum(onehot, axis=0, keepdims=True)
    carry_ref[...] = jnp.broadcast_to(carry, carry_ref.shape)
    cnt_ref[...] = jnp.broadcast_to(carry, cnt_ref.shape).astype(jnp.int32)
    r1 = jnp.sum(jnp.where(pick1, rank, 0.0), axis=1, keepdims=True).astype(jnp.int32)
    r2 = jnp.sum(jnp.where(pick2, rank, 0.0), axis=1, keepdims=True).astype(jnp.int32)
    meta_ref[...] = jnp.where(col == 0, i1, jnp.where(col == 1, i2, jnp.where(col == 2, r1, r2)))
    gw_ref[...] = jnp.where(col == 0, g1, g2)


def attn_out_router(a, w, resid, gain, w_router, tm=512):
    t, k = a.shape
    d = w.shape[1]
    wr = jnp.zeros((d, LANES), F32).at[:, :N_EXPERTS].set(w_router)
    return pl.pallas_call(
        _router_kernel,
        grid=(t // tm,),
        in_specs=[pl.BlockSpec((tm, k), lambda i: (i, 0)),
                  pl.BlockSpec((k, d), lambda i: (0, 0)),
                  pl.BlockSpec((tm, d), lambda i: (i, 0)),
                  pl.BlockSpec((1, d), lambda i: (0, 0)),
                  pl.BlockSpec((d, LANES), lambda i: (0, 0))],
        out_specs=[pl.BlockSpec((tm, d), lambda i: (i, 0)),
                   pl.BlockSpec((tm, d), lambda i: (i, 0)),
                   pl.BlockSpec((tm, LANES), lambda i: (i, 0)),
                   pl.BlockSpec((tm, LANES), lambda i: (i, 0)),
                   pl.BlockSpec((8, LANES), lambda i: (0, 0))],
        out_shape=[jax.ShapeDtypeStruct((t, d), F32),
                   jax.ShapeDtypeStruct((t, d), F32),
                   jax.ShapeDtypeStruct((t, LANES), jnp.int32),
                   jax.ShapeDtypeStruct((t, LANES), F32),
                   jax.ShapeDtypeStruct((8, LANES), jnp.int32)],
        scratch_shapes=[pltpu.VMEM((8, LANES), F32)],
        compiler_params=_cparams("arbitrary"),
        name="attn_out_router",
    )(a, w, resid, gain.reshape(1, d), wr)


def _dispatch_kernel(dest_ref, x_ref, buf_in_ref, buf_ref, sem):
    del buf_in_ref
    tb = x_ref.shape[0]

    def copy(r, slot):
        return pltpu.make_async_copy(x_ref.at[pl.ds(r, 1)], buf_ref.at[pl.ds(slot, 1)], sem)

    def issue(r, carry):
        copy(r, dest_ref[0, r]).start()
        copy(r, dest_ref[0, tb + r]).start()
        return carry

    def drain(r, carry):
        copy(r, dest_ref[0, r]).wait()
        copy(r, dest_ref[0, tb + r]).wait()
        return carry

    lax.fori_loop(0, tb, issue, 0)
    lax.fori_loop(0, tb, drain, 0)


def moe_dispatch(x, dest, n_slots, tb=256):
    t, d = x.shape
    nb = t // tb
    dest_blocks = jnp.concatenate([dest[0].reshape(nb, 1, tb), dest[1].reshape(nb, 1, tb)], axis=2)
    return pl.pallas_call(
        _dispatch_kernel,
        grid=(nb,),
        in_specs=[pl.BlockSpec((None, 1, 2 * tb), lambda i: (i, 0, 0), memory_space=pltpu.SMEM),
                  pl.BlockSpec((tb, d), lambda i: (i, 0)),
                  pl.BlockSpec(memory_space=pl.ANY)],
        out_specs=pl.BlockSpec(memory_space=pl.ANY),
        out_shape=jax.ShapeDtypeStruct((n_slots, d), x.dtype),
        scratch_shapes=[pltpu.SemaphoreType.DMA],
        input_output_aliases={2: 0},
        compiler_params=_cparams("arbitrary"),
        name="moe_dispatch",
    )(dest_blocks, x, jnp.zeros((n_slots, d), x.dtype))


def _expert_kernel(be_ref, x_ref, wg_ref, wu_ref, wd_ref, y_ref, xb_ref, acc_ref):
    del be_ref
    j = pl.program_id(1)

    @pl.when(j == 0)
    def _():
        xb_ref[...] = x_ref[...].astype(BF16)

    x = xb_ref[...]
    act = (_silu(_dot(x, wg_ref[...])) * _dot(x, wu_ref[...])).astype(BF16)
    part = _dot(act, wd_ref[...])

    @pl.when(j == 0)
    def _():
        acc_ref[...] = part

    @pl.when(j > 0)
    def _():
        acc_ref[...] += part

    @pl.when(j == pl.num_programs(1) - 1)
    def _():
        y_ref[...] = acc_ref[...]


def moe_experts(buf, block_expert, wg, wu, wd, tm=EXPERT_ROW_BLOCK, n_ff_tiles=2):
    n_slots, d = buf.shape
    ff = wg.shape[2]
    tf = ff // n_ff_tiles
    grid_spec = pltpu.PrefetchScalarGridSpec(
        num_scalar_prefetch=1,
        grid=(n_slots // tm, n_ff_tiles),
        in_specs=[pl.BlockSpec((tm, d), lambda i, j, be: (i, 0)),
                  pl.BlockSpec((None, d, tf), lambda i, j, be: (be[i], 0, j)),
                  pl.BlockSpec((None, d, tf), lambda i, j, be: (be[i], 0, j)),
                  pl.BlockSpec((None, tf, d), lambda i, j, be: (be[i], j, 0))],
        out_specs=pl.BlockSpec((tm, d), lambda i, j, be: (i, 0)),
        scratch_shapes=[pltpu.VMEM((tm, d), BF16), pltpu.VMEM((tm, d), F32)],
    )
    return pl.pallas_call(
        _expert_kernel,
        grid_spec=grid_spec,
        out_shape=jax.ShapeDtypeStruct((n_slots, d), F32),
        compiler_params=_cparams("parallel", "arbitrary"),
        name="moe_experts",
    )(block_expert, buf, wg, wu, wd)


def _combine_kernel(dest_ref, y_ref, h_ref, gw_ref, g_ref, o_ref, rows_ref, sem):
    tb = h_ref.shape[0]

    def copy(k, r, slot):
        return pltpu.make_async_copy(y_ref.at[pl.ds(slot, 1)], rows_ref.at[k, pl.ds(r, 1)], sem)

    def issue(r, carry):
        copy(0, r, dest_ref[0, r]).start()
        copy(1, r, dest_ref[0, tb + r]).start()
        return carry

    def drain(r, carry):
        copy(0, r, dest_ref[0, r]).wait()
        copy(1, r, dest_ref[0, tb + r]).wait()
        return carry

    lax.fori_loop(0, tb, issue, 0)
    lax.fori_loop(0, tb, drain, 0)
    gw = gw_ref[...]
    h = h_ref[...] + gw[:, 0:1] * rows_ref[0] + gw[:, 1:2] * rows_ref[1]
    o_ref[...] = _rms(h, g_ref[...])


def moe_combine_norm(y, dest, h, gw, gain, tb=256):
    t, d = h.shape
    nb = t // tb
    dest_blocks = jnp.concatenate([dest[0].reshape(nb, 1, tb), dest[1].reshape(nb, 1, tb)], axis=2)
    return pl.pallas_call(
        _combine_kernel,
        grid=(nb,),
        in_specs=[pl.BlockSpec((None, 1, 2 * tb), lambda i: (i, 0, 0), memory_space=pltpu.SMEM),
                  pl.BlockSpec(memory_space=pl.ANY),
                  pl.BlockSpec((tb, d), lambda i: (i, 0)),
                  pl.BlockSpec((tb, LANES), lambda i: (i, 0)),
                  pl.BlockSpec((1, d), lambda i: (0, 0))],
        out_specs=pl.BlockSpec((tb, d), lambda i: (i, 0)),
        out_shape=jax.ShapeDtypeStruct((t, d), F32),
        scratch_shapes=[pltpu.VMEM((TOP_K, tb, d), F32), pltpu.SemaphoreType.DMA],
        compiler_params=_cparams("arbitrary"),
        name="moe_combine_norm",
    )(dest_blocks, y, h, gw, gain.reshape(1, d))


def kernel(x, mix_norm, ffn_norm, even_w_in, even_conv_w, even_a_log, even_dt_bias, even_ret_norm,
           even_gdn_norm, even_w_out, odd_w_qkv, odd_w_out, ffn_w_gate, ffn_w_up, ffn_w_down,
           moe_router, moe_w_gate, moe_w_up, moe_w_down, final_norm):
    b, l, d = x.shape
    t = b * l
    x2 = x.reshape(t, d)

    n_main = 4 * D_RET + 4 * D_GDN
    w_in = even_w_in[0]
    w_main = w_in[:, :n_main].astype(BF16)
    w_small = jnp.zeros((d, LANES), BF16).at[:, :2 * N_GDN_HEADS].set(w_in[:, n_main:].astype(BF16))
    hn = rmsnorm(x2, mix_norm[0], BF16)
    proj = matmul(hn, w_main, F32)
    small = matmul(hn, w_small, F32)
    mixed = ret_gdn_mixer(proj.reshape(b, l, n_main), small.reshape(b, l, LANES),
                          even_conv_w[0], even_a_log[0], even_dt_bias[0],
                          even_ret_norm[0], even_gdn_norm[0])
    h1, hn1 = matmul_res_norm(mixed.reshape(t, d), even_w_out[0].astype(BF16), x2, ffn_norm[0])
    h2, hn2 = ffn_res_norm(hn1, h1, ffn_w_gate[0].astype(BF16), ffn_w_up[0].astype(BF16),
                           ffn_w_down[0].astype(BF16), mix_norm[1])

    n_heads = d // HEAD_DIM
    qkv = matmul(hn2, odd_w_qkv[0].astype(BF16), BF16)
    slopes = jnp.exp2(-8.0 * jnp.arange(1, n_heads + 1, dtype=F32) / n_heads)
    attn = moba_attention(qkv.reshape(b, l, 3 * d), slopes, n_heads)
    h3, hn3, meta, gw, counts = attn_out_router(attn.reshape(t, d), odd_w_out[0].astype(BF16), h2,
                                                ffn_norm[1], moe_router[0])

    rb = EXPERT_ROW_BLOCK
    n_assign = t * TOP_K
    n_blocks = -(-n_assign // rb) + N_EXPERTS
    cnt = counts[0, :N_EXPERTS]
    padded = (cnt + rb - 1) // rb * rb
    padded_end = jnp.cumsum(padded)
    padded_start = padded_end - padded
    dest = jnp.stack([padded_start[meta[:, 0]] + meta[:, 2],
                      padded_start[meta[:, 1]] + meta[:, 3]]).astype(jnp.int32)
    block_start = jnp.arange(n_blocks, dtype=jnp.int32) * rb
    block_expert = jnp.minimum(
        jnp.sum((padded_end[None, :] <= block_start[:, None]).astype(jnp.int32), axis=1),
        N_EXPERTS - 1)

    buf = moe_dispatch(hn3, dest, n_blocks * rb)
    y = moe_experts(buf, block_expert, moe_w_gate[0].astype(BF16), moe_w_up[0].astype(BF16),
                    moe_w_down[0].astype(BF16))
    out = moe_combine_norm(y, dest, h3, gw, final_norm)
    return out.reshape(b, l, d)
```

```python
import functools
import math

import jax
import jax.numpy as jnp
from jax import lax
from jax.experimental import pallas as pl
from jax.experimental.pallas import tpu as pltpu

F32 = jnp.float32
BF16 = jnp.bfloat16

HEAD_DIM = 128
N_RET_HEADS = 4
N_GDN_HEADS = 4
D_RET = N_RET_HEADS * HEAD_DIM
D_GDN = N_GDN_HEADS * HEAD_DIM
CONV_WIDTH = 4
MOBA_BLOCK = 256
MOBA_TOPK = 3
N_EXPERTS = 8
TOP_K = 2
EXPERT_ROW_BLOCK = 512
NORM_EPS = 1e-6
NEG_INF = -1e30

LANES = 128
SEQ_CHUNK = 256
MOBA_GROUP = 2
MOBA_STAGES = 4
VT_PAD = 16
LOG2_E = math.log2(math.e)
VMEM_LIMIT = 56 * 1024 * 1024


def _cparams(*sem):
    return pltpu.CompilerParams(dimension_semantics=sem, vmem_limit_bytes=VMEM_LIMIT)


def _dot(a, b):
    return jnp.dot(a, b, preferred_element_type=F32)


def _dot_nt(a, b):
    return lax.dot_general(a, b, (((1,), (1,)), ((), ())), preferred_element_type=F32)


def _dot_tn(a, b):
    return lax.dot_general(a, b, (((0,), (0,)), ((), ())), preferred_element_type=F32)


def _rms(x, gain):
    return x * lax.rsqrt(jnp.mean(x * x, axis=-1, keepdims=True) + NORM_EPS) * gain


def _silu(x):
    return x / (1.0 + jnp.exp(-x))


def _rmsnorm_kernel(x_ref, g_ref, o_ref):
    o_ref[...] = _rms(x_ref[...], g_ref[...]).astype(o_ref.dtype)


def rmsnorm(x, gain, out_dtype, tm=1024):
    t, d = x.shape
    return pl.pallas_call(
        _rmsnorm_kernel,
        grid=(t // tm,),
        in_specs=[pl.BlockSpec((tm, d), lambda i: (i, 0)),
                  pl.BlockSpec((1, d), lambda i: (0, 0))],
        out_specs=pl.BlockSpec((tm, d), lambda i: (i, 0)),
        out_shape=jax.ShapeDtypeStruct((t, d), out_dtype),
        compiler_params=_cparams("parallel"),
        name="rmsnorm",
    )(x, gain.reshape(1, d))


def _matmul_kernel(a_ref, w_ref, o_ref):
    o_ref[...] = _dot(a_ref[...], w_ref[...]).astype(o_ref.dtype)


def matmul(a, w, out_dtype, tm=1024, tn=1024):
    t, k = a.shape
    n = w.shape[1]
    tn = min(tn, n)
    return pl.pallas_call(
        _matmul_kernel,
        grid=(t // tm, n // tn),
        in_specs=[pl.BlockSpec((tm, k), lambda i, j: (i, 0)),
                  pl.BlockSpec((k, tn), lambda i, j: (0, j))],
        out_specs=pl.BlockSpec((tm, tn), lambda i, j: (i, j)),
        out_shape=jax.ShapeDtypeStruct((t, n), out_dtype),
        compiler_params=_cparams("parallel", "parallel"),
        name="matmul",
    )(a, w)


def _matmul_res_norm_kernel(a_ref, w_ref, r_ref, g_ref, h_ref, hn_ref):
    h = r_ref[...] + _dot(a_ref[...], w_ref[...])
    h_ref[...] = h
    hn_ref[...] = _rms(h, g_ref[...]).astype(hn_ref.dtype)


def matmul_res_norm(a, w, resid, gain, tm=512):
    t, k = a.shape
    d = w.shape[1]
    return pl.pallas_call(
        _matmul_res_norm_kernel,
        grid=(t // tm,),
        in_specs=[pl.BlockSpec((tm, k), lambda i: (i, 0)),
                  pl.BlockSpec((k, d), lambda i: (0, 0)),
                  pl.BlockSpec((tm, d), lambda i: (i, 0)),
                  pl.BlockSpec((1, d), lambda i: (0, 0))],
        out_specs=[pl.BlockSpec((tm, d), lambda i: (i, 0)),
                   pl.BlockSpec((tm, d), lambda i: (i, 0))],
        out_shape=[jax.ShapeDtypeStruct((t, d), F32),
                   jax.ShapeDtypeStruct((t, d), BF16)],
        compiler_params=_cparams("parallel"),
        name="matmul_res_norm",
    )(a, w, resid, gain.reshape(1, d))


def _ffn_kernel(x_ref, r_ref, wg_ref, wu_ref, wd_ref, g_ref, h_ref, hn_ref, acc_ref):
    j = pl.program_id(1)

    @pl.when(j == 0)
    def _():
        acc_ref[...] = r_ref[...]

    x = x_ref[...]
    act = (_silu(_dot(x, wg_ref[...])) * _dot(x, wu_ref[...])).astype(BF16)
    acc_ref[...] += _dot(act, wd_ref[...])

    @pl.when(j == pl.num_programs(1) - 1)
    def _():
        h = acc_ref[...]
        h_ref[...] = h
        hn_ref[...] = _rms(h, g_ref[...]).astype(hn_ref.dtype)


def ffn_res_norm(x, resid, wg, wu, wd, gain, tm=512, n_ff_tiles=2):
    t, d = x.shape
    ff = wg.shape[1]
    tf = ff // n_ff_tiles
    return pl.pallas_call(
        _ffn_kernel,
        grid=(t // tm, n_ff_tiles),
        in_specs=[pl.BlockSpec((tm, d), lambda i, j: (i, 0)),
                  pl.BlockSpec((tm, d), lambda i, j: (i, 0)),
                  pl.BlockSpec((d, tf), lambda i, j: (0, j)),
                  pl.BlockSpec((d, tf), lambda i, j: (0, j)),
                  pl.BlockSpec((tf, d), lambda i, j: (j, 0)),
                  pl.BlockSpec((1, d), lambda i, j: (0, 0))],
        out_specs=[pl.BlockSpec((tm, d), lambda i, j: (i, 0)),
                   pl.BlockSpec((tm, d), lambda i, j: (i, 0))],
        out_shape=[jax.ShapeDtypeStruct((t, d), F32),
                   jax.ShapeDtypeStruct((t, d), BF16)],
        scratch_shapes=[pltpu.VMEM((tm, d), F32)],
        compiler_params=_cparams("parallel", "arbitrary"),
        name="ffn_res_norm",
    )(x, resid, wg, wu, wd, gain.reshape(1, d))


def _mixer_kernel(p_ref, s_ref, cw_ref, alog_ref, dtb_ref, rn_ref, gn_ref, o_ref,
                  ext_ref, ret_state, gdn_state, lvl_ref):
    c = SEQ_CHUNK
    d = HEAD_DIM
    step = pl.program_id(1)
    row = lax.broadcasted_iota(jnp.int32, (c, c), 0)
    col = lax.broadcasted_iota(jnp.int32, (c, c), 1)
    diff = (row - col).astype(F32)
    incl = row >= col
    strict = row > col
    pos = lax.broadcasted_iota(jnp.int32, (c, 1), 0).astype(F32)

    @pl.when(step == 0)
    def _():
        ret_state[...] = jnp.zeros_like(ret_state)
        gdn_state[...] = jnp.zeros_like(gdn_state)
        ext_ref[0:8, :] = jnp.zeros((8, ext_ref.shape[1]), F32)
        for k in range(8):
            rb = row >> k
            cb = col >> k
            lvl_ref[k] = jnp.where((rb - cb == 1) & ((cb & 1) == 0), 1.0, 0.0)

    for h in range(N_RET_HEADS):
        lg = math.log1p(-2.0 ** (-5.0 - h))
        q = p_ref[:, h * d:(h + 1) * d]
        k = p_ref[:, D_RET + h * d:D_RET + (h + 1) * d] * d ** -0.5
        v = p_ref[:, 2 * D_RET + h * d:2 * D_RET + (h + 1) * d]
        gate = p_ref[:, 3 * D_RET + h * d:3 * D_RET + (h + 1) * d]
        d_mask = jnp.where(incl, jnp.exp(lg * jnp.maximum(diff, 0.0)), 0.0)
        vb = v.astype(BF16)
        scores = _dot_nt(q.astype(BF16), k.astype(BF16)) * d_mask
        state = ret_state[h]
        q_dec = q * jnp.exp(lg * (pos + 1.0))
        o = _dot(scores.astype(BF16), vb) + _dot(q_dec.astype(BF16), state.astype(BF16))
        k_dec = k * jnp.exp(lg * (c - 1.0 - pos))
        ret_state[h] = state * math.exp(lg * c) + _dot_tn(k_dec.astype(BF16), vb)
        o = _rms(o, rn_ref[h:h + 1, :]) * _silu(gate)
        o_ref[:, h * d:(h + 1) * d] = o.astype(o_ref.dtype)

    base = 4 * D_RET
    ext_ref[8:8 + c, :] = p_ref[:, base:base + 3 * D_GDN]

    def conv_silu(lo):
        acc = cw_ref[0:1, lo:lo + d] * ext_ref[5:5 + c, lo:lo + d]
        for t in range(1, CONV_WIDTH):
            acc += cw_ref[t:t + 1, lo:lo + d] * ext_ref[5 + t:5 + t + c, lo:lo + d]
        return _silu(acc)

    def l2n(x):
        return x * lax.rsqrt(jnp.sum(x * x, axis=-1, keepdims=True) + NORM_EPS)

    small = s_ref[...]
    tri_incl = jnp.where(incl, 1.0, 0.0)
    for h in range(N_GDN_HEADS):
        q = l2n(conv_silu(h * d)) * d ** -0.5
        k = l2n(conv_silu(D_GDN + h * d))
        v = conv_silu(2 * D_GDN + h * d)
        gate = p_ref[:, base + 3 * D_GDN + h * d:base + 3 * D_GDN + (h + 1) * d]
        beta = 1.0 / (1.0 + jnp.exp(-small[:, h:h + 1]))
        a_raw = small[:, N_GDN_HEADS + h:N_GDN_HEADS + h + 1] + dtb_ref[h]
        softplus = jnp.maximum(a_raw, 0.0) + jnp.log1p(jnp.exp(-jnp.abs(a_raw)))
        g = -jnp.exp(jnp.full((1, 1), alog_ref[h], F32)) * softplus
        gmat = jnp.where(strict, jnp.broadcast_to(g, (c, c)), 0.0)
        gdiff = lax.dot_general(tri_incl, gmat, (((1,), (0,)), ((), ())),
                                precision=lax.Precision.HIGHEST, preferred_element_type=F32)
        gcum = gdiff[:, 0:1] + g[0:1, 0:1]
        decay = jnp.where(incl, jnp.exp(jnp.where(incl, gdiff, 0.0)), 0.0)
        kb = k.astype(BF16)
        k_beta = k * beta
        lower = jnp.where(strict, _dot_nt(k_beta.astype(BF16), kb) * decay, 0.0)
        t_mat = jnp.where(row == col, 1.0, 0.0)
        for lv in range(8):
            off = (lower * lvl_ref[lv]).astype(BF16)
            tb = t_mat.astype(BF16)
            t_mat = t_mat - _dot(tb, _dot(off, tb).astype(BF16))
        e_g = jnp.exp(gcum)
        rhs = jnp.concatenate([v * beta, k_beta * e_g], axis=1).astype(BF16)
        uw = _dot(t_mat.astype(BF16), rhs)
        u = uw[:, :d]
        w = uw[:, d:]
        attn = jnp.where(incl, _dot_nt(q.astype(BF16), kb) * decay, 0.0)
        state = gdn_state[h]
        sb = state.astype(BF16)
        g_last = gcum[c - 1:c, :]
        v_new = u - _dot(w.astype(BF16), sb)
        vnb = v_new.astype(BF16)
        o = _dot((q * e_g).astype(BF16), sb) + _dot(attn.astype(BF16), vnb)
        k_dec = k * jnp.exp(g_last - gcum)
        gdn_state[h] = state * jnp.exp(g_last) + _dot_tn(k_dec.astype(BF16), vnb)
        o = _rms(o, gn_ref[h:h + 1, :]) * _silu(gate)
        o_ref[:, D_RET + h * d:D_RET + (h + 1) * d] = o.astype(o_ref.dtype)

    ext_ref[0:8, :] = ext_ref[c:c + 8, :]


def ret_gdn_mixer(proj, small, conv_w, a_log, dt_bias, ret_norm, gdn_norm):
    b, l, width = proj.shape
    c = SEQ_CHUNK
    smem = pl.BlockSpec(memory_space=pltpu.SMEM)
    return pl.pallas_call(
        _mixer_kernel,
        grid=(b, l // c),
        in_specs=[pl.BlockSpec((None, c, width), lambda i, j: (i, j, 0)),
                  pl.BlockSpec((None, c, LANES), lambda i, j: (i, j, 0)),
                  pl.BlockSpec((CONV_WIDTH, 3 * D_GDN), lambda i, j: (0, 0)),
                  smem, smem,
                  pl.BlockSpec((N_RET_HEADS, HEAD_DIM), lambda i, j: (0, 0)),
                  pl.BlockSpec((N_GDN_HEADS, HEAD_DIM), lambda i, j: (0, 0))],
        out_specs=pl.BlockSpec((None, c, D_RET + D_GDN), lambda i, j: (i, j, 0)),
        out_shape=jax.ShapeDtypeStruct((b, l, D_RET + D_GDN), BF16),
        scratch_shapes=[pltpu.VMEM((c + 8, 3 * D_GDN), F32),
                        pltpu.VMEM((N_RET_HEADS, HEAD_DIM, HEAD_DIM), F32),
                        pltpu.VMEM((N_GDN_HEADS, HEAD_DIM, HEAD_DIM), F32),
                        pltpu.VMEM((8, c, c), F32)],
        compiler_params=_cparams("arbitrary", "arbitrary"),
        name="ret_gdn_mixer",
    )(proj, small, conv_w, a_log, dt_bias, ret_norm, gdn_norm)


def _moba_kernel(slope_ref, q_ref, k_ref, v_ref, o_ref,
                 kmean_ref, vt_ref, bias_ref, sel_ref, sa_ref, sb_ref):
    blk = MOBA_BLOCK
    grp = MOBA_GROUP
    d = HEAD_DIM
    n_blk = kmean_ref.shape[0]
    h = pl.program_id(1)
    i = pl.program_id(2)
    slope2 = slope_ref[h] * LOG2_E
    key_pos = lax.broadcasted_iota(jnp.int32, (blk, blk), 0)
    qry_pos = lax.broadcasted_iota(jnp.int32, (blk, blk), 1)

    @pl.when(i == 0)
    def _():
        ones_row = jnp.where(lax.broadcasted_iota(jnp.int32, (VT_PAD, blk), 0) == 0,
                             1.0, 0.0).astype(BF16)

        def setup(j, carry):
            lo = pl.multiple_of(j * blk, blk)
            kmean_ref[pl.ds(j, 1), :] = jnp.mean(k_ref[pl.ds(lo, blk), :].astype(F32),
                                                 axis=0, keepdims=True)
            vt_ref[j, 0:d, :] = v_ref[pl.ds(lo, blk), :].astype(F32).T.astype(BF16)
            vt_ref[j, d:d + VT_PAD, :] = ones_row
            return carry
        lax.fori_loop(0, n_blk, setup, 0)
        bias_ref[...] = -slope2 * (qry_pos - key_pos).astype(F32)

    q_t = q_ref[...].astype(F32).T
    q_ts = (q_t * (d ** -0.5 * LOG2_E)).astype(BF16)

    blk_id = lax.broadcasted_iota(jnp.int32, (n_blk, blk), 0)
    gate = lax.dot_general(kmean_ref[...], q_t, (((1,), (0,)), ((), ())),
                           precision=lax.Precision.HIGHEST, preferred_element_type=F32)
    gate = jnp.where(blk_id < i, gate, NEG_INF)
    sel = jnp.zeros((n_blk, blk), F32)
    for _ in range(min(MOBA_TOPK, n_blk)):
        best = jnp.max(gate, axis=0, keepdims=True)
        first = jnp.min(jnp.where(gate == best, blk_id, n_blk), axis=0, keepdims=True)
        pick = blk_id == first
        sel = jnp.where(pick & (blk_id < i), 1.0, sel)
        gate = jnp.where(pick, -jnp.inf, gate)
    sel_ref[...] = sel

    start = pl.multiple_of(i * blk, blk)
    s = _dot(k_ref[pl.ds(start, blk), :], q_ts) + bias_ref[...]
    s = jnp.where(qry_pos >= key_pos, s, NEG_INF)
    m0 = jnp.max(s, axis=0, keepdims=True)
    acc0 = _dot(vt_ref[i], jnp.exp2(s - m0).astype(BF16))

    last_group = n_blk // grp - 1

    def score(g, buf):
        g = jnp.minimum(g, last_group)
        peaks = []
        for w in range(grp):
            lo = pl.multiple_of((g * grp + w) * blk, blk)
            s_w = _dot(k_ref[pl.ds(lo, blk), :], q_ts) + bias_ref[...]
            buf[w] = s_w
            peaks.append(jnp.max(s_w, axis=0, keepdims=True))
        return tuple(peaks)

    def absorb(g, buf, peaks, m, acc):
        m_new = m
        shifts = []
        for w in range(grp):
            j = g * grp + w
            chosen = (sel_ref[pl.ds(jnp.minimum(j, n_blk - 1), 1), :] > 0.0) & (j < i)
            offset = slope2 * ((j - i) * blk).astype(F32)
            m_new = jnp.maximum(m_new, jnp.where(chosen, peaks[w] + offset, -jnp.inf))
            shifts.append((chosen, offset))
        acc = jnp.exp2(m - m_new) * acc
        for w in range(grp):
            chosen, offset = shifts[w]
            p = jnp.exp2(buf[w] - jnp.where(chosen, m_new - offset, -NEG_INF))
            acc = acc + _dot(vt_ref[jnp.minimum(g * grp + w, n_blk - 1)], p.astype(BF16))
        return m_new, acc

    def body(t, carry):
        m, acc, peaks = carry
        g = MOBA_STAGES * t
        for u in range(0, MOBA_STAGES, 2):
            peaks_b = score(g + u + 1, sb_ref)
            m, acc = absorb(g + u, sa_ref, peaks, m, acc)
            peaks = score(g + u + 2, sa_ref)
            m, acc = absorb(g + u + 1, sb_ref, peaks_b, m, acc)
        return m, acc, peaks

    n_groups = (i + grp - 1) // grp
    _, acc, _ = lax.fori_loop(0, (n_groups + MOBA_STAGES - 1) // MOBA_STAGES, body,
                              (m0, acc0, score(0, sa_ref)))
    o_ref[...] = (acc[0:d, :] / acc[d:d + 1, :]).T.astype(o_ref.dtype)


def moba_attention(qkv, slopes, n_heads):
    b, l, _ = qkv.shape
    d = HEAD_DIM
    blk = MOBA_BLOCK
    n_blk = l // blk
    assert l % blk == 0 and n_blk % MOBA_GROUP == 0
    return pl.pallas_call(
        _moba_kernel,
        grid=(b, n_heads, n_blk),
        in_specs=[pl.BlockSpec(memory_space=pltpu.SMEM),
                  pl.BlockSpec((None, blk, d), lambda bi, h, i: (bi, i, h)),
                  pl.BlockSpec((None, l, d), lambda bi, h, i: (bi, 0, n_heads + h)),
                  pl.BlockSpec((None, l, d), lambda bi, h, i: (bi, 0, 2 * n_heads + h))],
        out_specs=pl.BlockSpec((None, blk, d), lambda bi, h, i: (bi, i, h)),
        out_shape=jax.ShapeDtypeStruct((b, l, n_heads * d), BF16),
        scratch_shapes=[pltpu.VMEM((n_blk, d), F32),
                        pltpu.VMEM((n_blk, d + VT_PAD, blk), BF16),
                        pltpu.VMEM((blk, blk), F32),
                        pltpu.VMEM((n_blk, blk), F32),
                        pltpu.VMEM((MOBA_GROUP, blk, blk), F32),
                        pltpu.VMEM((MOBA_GROUP, blk, blk), F32)],
        compiler_params=_cparams("arbitrary", "arbitrary", "arbitrary"),
        name="moba_attention",
    )(slopes, qkv, qkv, qkv)


def _router_kernel(a_ref, w_ref, r_ref, g_ref, wr_ref, h_ref, hn_ref, meta_ref, gw_ref,
                   cnt_ref, carry_ref):
    tm = a_ref.shape[0]

    @pl.when(pl.program_id(0) == 0)
    def _():
        carry_ref[...] = jnp.zeros_like(carry_ref)

    h = r_ref[...] + _dot(a_ref[...], w_ref[...])
    h_ref[...] = h
    hn = _rms(h, g_ref[...])
    hn_ref[...] = hn
    logits = lax.dot_general(hn, wr_ref[...], (((1,), (0,)), ((), ())),
                             precision=lax.Precision.HIGHEST, preferred_element_type=F32)
    col = lax.broadcasted_iota(jnp.int32, (tm, LANES), 1)
    logits = jnp.where(col < N_EXPERTS, logits, -jnp.inf)
    m1 = jnp.max(logits, axis=1, keepdims=True)
    i1 = jnp.min(jnp.where(logits == m1, col, LANES), axis=1, keepdims=True)
    rest = jnp.where(col == i1, -jnp.inf, logits)
    m2 = jnp.max(rest, axis=1, keepdims=True)
    i2 = jnp.min(jnp.where(rest == m2, col, LANES), axis=1, keepdims=True)
    e = jnp.exp(m2 - m1)
    g1 = 1.0 / (1.0 + e)
    g2 = e / (1.0 + e)
    pick1 = col == i1
    pick2 = col == i2
    onehot = jnp.where(pick1 | pick2, 1.0, 0.0)
    tri = jnp.where(lax.broadcasted_iota(jnp.int32, (tm, tm), 0)
                    > lax.broadcasted_iota(jnp.int32, (tm, tm), 1), 1.0, 0.0).astype(BF16)
    rank = _dot(tri, onehot.astype(BF16)) + carry_ref[0:1, :]
    carry = carry_ref[0:1, :] + jnp.sum(onehot, axis=0, keepdims=True)
    carry_ref[...] = jnp.broadcast_to(carry, carry_ref.shape)
    cnt_ref[...] = jnp.broadcast_to(carry, cnt_ref.shape).astype(jnp.int32)
    r1 = jnp.sum(jnp.where(pick1, rank, 0.0), axis=1, keepdims=True).astype(jnp.int32)
    r2 = jnp.sum(jnp.where(pick2, rank, 0.0), axis=1, keepdims=True).astype(jnp.int32)
    meta_ref[...] = jnp.where(col == 0, i1, jnp.where(col == 1, i2, jnp.where(col == 2, r1, r2)))
    gw_ref[...] = jnp.where(col == 0, g1, g2)


def attn_out_router(a, w, resid, gain, w_router, tm=512):
    t, k = a.shape
    d = w.shape[1]
    wr = jnp.zeros((d, LANES), F32).at[:, :N_EXPERTS].set(w_router)
    return pl.pallas_call(
        _router_kernel,
        grid=(t // tm,),
        in_specs=[pl.BlockSpec((tm, k), lambda i: (i, 0)),
                  pl.BlockSpec((k, d), lambda i: (0, 0)),
                  pl.BlockSpec((tm, d), lambda i: (i, 0)),
                  pl.BlockSpec((1, d), lambda i: (0, 0)),
                  pl.BlockSpec((d, LANES), lambda i: (0, 0))],
        out_specs=[pl.BlockSpec((tm, d), lambda i: (i, 0)),
                   pl.BlockSpec((tm, d), lambda i: (i, 0)),
                   pl.BlockSpec((tm, LANES), lambda i: (i, 0)),
                   pl.BlockSpec((tm, LANES), lambda i: (i, 0)),
                   pl.BlockSpec((8, LANES), lambda i: (0, 0))],
        out_shape=[jax.ShapeDtypeStruct((t, d), F32),
                   jax.ShapeDtypeStruct((t, d), F32),
                   jax.ShapeDtypeStruct((t, LANES), jnp.int32),
                   jax.ShapeDtypeStruct((t, LANES), F32),
                   jax.ShapeDtypeStruct((8, LANES), jnp.int32)],
        scratch_shapes=[pltpu.VMEM((8, LANES), F32)],
        compiler_params=_cparams("arbitrary"),
        name="attn_out_router",
    )(a, w, resid, gain.reshape(1, d), wr)


def _dispatch_kernel(dest_ref, x_ref, buf_in_ref, buf_ref, sem):
    del buf_in_ref
    tb = x_ref.shape[0]

    def copy(r, slot):
        return pltpu.make_async_copy(x_ref.at[pl.ds(r, 1)], buf_ref.at[pl.ds(slot, 1)], sem)

    def issue(r, carry):
        copy(r, dest_ref[0, r]).start()
        copy(r, dest_ref[0, tb + r]).start()
        return carry

    def drain(r, carry):
        copy(r, dest_ref[0, r]).wait()
        copy(r, dest_ref[0, tb + r]).wait()
        return carry

    lax.fori_loop(0, tb, issue, 0)
    lax.fori_loop(0, tb, drain, 0)


def moe_dispatch(x, dest, n_slots, tb=256):
    t, d = x.shape
    nb = t // tb
    dest_blocks = jnp.concatenate([dest[0].reshape(nb, 1, tb), dest[1].reshape(nb, 1, tb)], axis=2)
    return pl.pallas_call(
        _dispatch_kernel,
        grid=(nb,),
        in_specs=[pl.BlockSpec((None, 1, 2 * tb), lambda i: (i, 0, 0), memory_space=pltpu.SMEM),
                  pl.BlockSpec((tb, d), lambda i: (i, 0)),
                  pl.BlockSpec(memory_space=pl.ANY)],
        out_specs=pl.BlockSpec(memory_space=pl.ANY),
        out_shape=jax.ShapeDtypeStruct((n_slots, d), x.dtype),
        scratch_shapes=[pltpu.SemaphoreType.DMA],
        input_output_aliases={2: 0},
        compiler_params=_cparams("arbitrary"),
        name="moe_dispatch",
    )(dest_blocks, x, jnp.zeros((n_slots, d), x.dtype))


def _expert_kernel(be_ref, x_ref, wg_ref, wu_ref, wd_ref, y_ref, xb_ref, acc_ref):
    del be_ref
    j = pl.program_id(1)

    @pl.when(j == 0)
    def _():
        xb_ref[...] = x_ref[...].astype(BF16)

    x = xb_ref[...]
    act = (_silu(_dot(x, wg_ref[...])) * _dot(x, wu_ref[...])).astype(BF16)
    part = _dot(act, wd_ref[...])

    @pl.when(j == 0)
    def _():
        acc_ref[...] = part

    @pl.when(j > 0)
    def _():
        acc_ref[...] += part

    @pl.when(j == pl.num_programs(1) - 1)
    def _():
        y_ref[...] = acc_ref[...]


def moe_experts(buf, block_expert, wg, wu, wd, tm=EXPERT_ROW_BLOCK, n_ff_tiles=2):
    n_slots, d = buf.shape
    ff = wg.shape[2]
    tf = ff // n_ff_tiles
    grid_spec = pltpu.PrefetchScalarGridSpec(
        num_scalar_prefetch=1,
        grid=(n_slots // tm, n_ff_tiles),
        in_specs=[pl.BlockSpec((tm, d), lambda i, j, be: (i, 0)),
                  pl.BlockSpec((None, d, tf), lambda i, j, be: (be[i], 0, j)),
                  pl.BlockSpec((None, d, tf), lambda i, j, be: (be[i], 0, j)),
                  pl.BlockSpec((None, tf, d), lambda i, j, be: (be[i], j, 0))],
        out_specs=pl.BlockSpec((tm, d), lambda i, j, be: (i, 0)),
        scratch_shapes=[pltpu.VMEM((tm, d), BF16), pltpu.VMEM((tm, d), F32)],
    )
    return pl.pallas_call(
        _expert_kernel,
        grid_spec=grid_spec,
        out_shape=jax.ShapeDtypeStruct((n_slots, d), F32),
        compiler_params=_cparams("parallel", "arbitrary"),
        name="moe_experts",
    )(block_expert, buf, wg, wu, wd)


def _combine_kernel(dest_ref, y_ref, h_ref, gw_ref, g_ref, o_ref, rows_ref, sem):
    tb = h_ref.shape[0]

    def copy(k, r, slot):
        return pltpu.make_async_copy(y_ref.at[pl.ds(slot, 1)], rows_ref.at[k, pl.ds(r, 1)], sem)

    def issue(r, carry):
        copy(0, r, dest_ref[0, r]).start()
        copy(1, r, dest_ref[0, tb + r]).start()
        return carry

    def drain(r, carry):
        copy(0, r, dest_ref[0, r]).wait()
        copy(1, r, dest_ref[0, tb + r]).wait()
        return carry

    lax.fori_loop(0, tb, issue, 0)
    lax.fori_loop(0, tb, drain, 0)
    gw = gw_ref[...]
    h = h_ref[...] + gw[:, 0:1] * rows_ref[0] + gw[:, 1:2] * rows_ref[1]
    o_ref[...] = _rms(h, g_ref[...])


def moe_combine_norm(y, dest, h, gw, gain, tb=256):
    t, d = h.shape
    nb = t // tb
    dest_blocks = jnp.concatenate([dest[0].reshape(nb, 1, tb), dest[1].reshape(nb, 1, tb)], axis=2)
    return pl.pallas_call(
        _combine_kernel,
        grid=(nb,),
        in_specs=[pl.BlockSpec((None, 1, 2 * tb), lambda i: (i, 0, 0), memory_space=pltpu.SMEM),
                  pl.BlockSpec(memory_space=pl.ANY),
                  pl.BlockSpec((tb, d), lambda i: (i, 0)),
                  pl.BlockSpec((tb, LANES), lambda i: (i, 0)),
                  pl.BlockSpec((1, d), lambda i: (0, 0))],
        out_specs=pl.BlockSpec((tb, d), lambda i: (i, 0)),
        out_shape=jax.ShapeDtypeStruct((t, d), F32),
        scratch_shapes=[pltpu.VMEM((TOP_K, tb, d), F32), pltpu.SemaphoreType.DMA],
        compiler_params=_cparams("arbitrary"),
        name="moe_combine_norm",
    )(dest_blocks, y, h, gw, gain.reshape(1, d))


def kernel(x, mix_norm, ffn_norm, even_w_in, even_conv_w, even_a_log, even_dt_bias, even_ret_norm,
           even_gdn_norm, even_w_out, odd_w_qkv, odd_w_out, ffn_w_gate, ffn_w_up, ffn_w_down,
           moe_router, moe_w_gate, moe_w_up, moe_w_down, final_norm):
    b, l, d = x.shape
    t = b * l
    x2 = x.reshape(t, d)

    n_main = 4 * D_RET + 4 * D_GDN
    w_in = even_w_in[0]
    w_main = w_in[:, :n_main].astype(BF16)
    w_small = jnp.zeros((d, LANES), BF16).at[:, :2 * N_GDN_HEADS].set(w_in[:, n_main:].astype(BF16))
    hn = rmsnorm(x2, mix_norm[0], BF16)
    proj = matmul(hn, w_main, F32)
    small = matmul(hn, w_small, F32)
    mixed = ret_gdn_mixer(proj.reshape(b, l, n_main), small.reshape(b, l, LANES),
                          even_conv_w[0], even_a_log[0], even_dt_bias[0],
                          even_ret_norm[0], even_gdn_norm[0])
    h1, hn1 = matmul_res_norm(mixed.reshape(t, d), even_w_out[0].astype(BF16), x2, ffn_norm[0])
    h2, hn2 = ffn_res_norm(hn1, h1, ffn_w_gate[0].astype(BF16), ffn_w_up[0].astype(BF16),
                           ffn_w_down[0].astype(BF16), mix_norm[1])

    n_heads = d // HEAD_DIM
    qkv = matmul(hn2, odd_w_qkv[0].astype(BF16), BF16)
    slopes = jnp.exp2(-8.0 * jnp.arange(1, n_heads + 1, dtype=F32) / n_heads)
    attn = moba_attention(qkv.reshape(b, l, 3 * d), slopes, n_heads)
    h3, hn3, meta, gw, counts = attn_out_router(attn.reshape(t, d), odd_w_out[0].astype(BF16), h2,
                                                ffn_norm[1], moe_router[0])

    rb = EXPERT_ROW_BLOCK
    n_assign = t * TOP_K
    n_blocks = -(-n_assign // rb) + N_EXPERTS
    cnt = counts[0, :N_EXPERTS]
    padded = (cnt + rb - 1) // rb * rb
    padded_end = jnp.cumsum(padded)
    padded_start = padded_end - padded
    dest = jnp.stack([padded_start[meta[:, 0]] + meta[:, 2],
                      padded_start[meta[:, 1]] + meta[:, 3]]).astype(jnp.int32)
    block_start = jnp.arange(n_blocks, dtype=jnp.int32) * rb
    block_expert = jnp.minimum(
        jnp.sum((padded_end[None, :] <= block_start[:, None]).astype(jnp.int32), axis=1),
        N_EXPERTS - 1)

    buf = moe_dispatch(hn3, dest, n_blocks * rb)
    y = moe_experts(buf, block_expert, moe_w_gate[0].astype(BF16), moe_w_up[0].astype(BF16),
                    moe_w_down[0].astype(BF16))
    out = moe_combine_norm(y, dest, h3, gw, final_norm)
    return out.reshape(b, l, d)
```

```python
import functools
import math

import jax
import jax.numpy as jnp
from jax import lax
from jax.experimental import pallas as pl
from jax.experimental.pallas import tpu as pltpu

F32 = jnp.float32
BF16 = jnp.bfloat16

HEAD_DIM = 128
N_RET_HEADS = 4
N_GDN_HEADS = 4
D_RET = N_RET_HEADS * HEAD_DIM
D_GDN = N_GDN_HEADS * HEAD_DIM
CONV_WIDTH = 4
MOBA_BLOCK = 256
MOBA_TOPK = 3
N_EXPERTS = 8
TOP_K = 2
EXPERT_ROW_BLOCK = 512
NORM_EPS = 1e-6
NEG_INF = -1e30

LANES = 128
SEQ_CHUNK = 256
MOBA_GROUP = 2
MOBA_STAGES = 4
VT_PAD = 16
LOG2_E = math.log2(math.e)
VMEM_LIMIT = 56 * 1024 * 1024


def _cparams(*sem):
    return pltpu.CompilerParams(dimension_semantics=sem, vmem_limit_bytes=VMEM_LIMIT)


def _dot(a, b):
    return jnp.dot(a, b, preferred_element_type=F32)


def _dot_nt(a, b):
    return lax.dot_general(a, b, (((1,), (1,)), ((), ())), preferred_element_type=F32)


def _dot_tn(a, b):
    return lax.dot_general(a, b, (((0,), (0,)), ((), ())), preferred_element_type=F32)


def _rms(x, gain):
    return x * lax.rsqrt(jnp.mean(x * x, axis=-1, keepdims=True) + NORM_EPS) * gain


def _sigmoid(x):
    return 0.5 + 0.5 * jnp.tanh(0.5 * x)


def _silu(x):
    half = 0.5 * x
    return half + half * jnp.tanh(half)


def _rmsnorm_kernel(x_ref, g_ref, o_ref):
    o_ref[...] = _rms(x_ref[...], g_ref[...]).astype(o_ref.dtype)


def rmsnorm(x, gain, out_dtype, tm=1024):
    t, d = x.shape
    return pl.pallas_call(
        _rmsnorm_kernel,
        grid=(t // tm,),
        in_specs=[pl.BlockSpec((tm, d), lambda i: (i, 0)),
                  pl.BlockSpec((1, d), lambda i: (0, 0))],
        out_specs=pl.BlockSpec((tm, d), lambda i: (i, 0)),
        out_shape=jax.ShapeDtypeStruct((t, d), out_dtype),
        compiler_params=_cparams("parallel"),
        name="rmsnorm",
    )(x, gain.reshape(1, d))


def _matmul_kernel(a_ref, w_ref, o_ref):
    o_ref[...] = _dot(a_ref[...], w_ref[...]).astype(o_ref.dtype)


def matmul(a, w, out_dtype, tm=1024, tn=1024):
    t, k = a.shape
    n = w.shape[1]
    tn = min(tn, n)
    return pl.pallas_call(
        _matmul_kernel,
        grid=(t // tm, n // tn),
        in_specs=[pl.BlockSpec((tm, k), lambda i, j: (i, 0)),
                  pl.BlockSpec((k, tn), lambda i, j: (0, j))],
        out_specs=pl.BlockSpec((tm, tn), lambda i, j: (i, j)),
        out_shape=jax.ShapeDtypeStruct((t, n), out_dtype),
        compiler_params=_cparams("parallel", "parallel"),
        name="matmul",
    )(a, w)


def _matmul_res_norm_kernel(a_ref, w_ref, r_ref, g_ref, h_ref, hn_ref):
    h = r_ref[...] + _dot(a_ref[...], w_ref[...])
    h_ref[...] = h
    hn_ref[...] = _rms(h, g_ref[...]).astype(hn_ref.dtype)


def matmul_res_norm(a, w, resid, gain, tm=512):
    t, k = a.shape
    d = w.shape[1]
    return pl.pallas_call(
        _matmul_res_norm_kernel,
        grid=(t // tm,),
        in_specs=[pl.BlockSpec((tm, k), lambda i: (i, 0)),
                  pl.BlockSpec((k, d), lambda i: (0, 0)),
                  pl.BlockSpec((tm, d), lambda i: (i, 0)),
                  pl.BlockSpec((1, d), lambda i: (0, 0))],
        out_specs=[pl.BlockSpec((tm, d), lambda i: (i, 0)),
                   pl.BlockSpec((tm, d), lambda i: (i, 0))],
        out_shape=[jax.ShapeDtypeStruct((t, d), F32),
                   jax.ShapeDtypeStruct((t, d), BF16)],
        compiler_params=_cparams("parallel"),
        name="matmul_res_norm",
    )(a, w, resid, gain.reshape(1, d))


def _ffn_kernel(x_ref, r_ref, wg_ref, wu_ref, wd_ref, g_ref, h_ref, hn_ref, acc_ref):
    j = pl.program_id(1)

    @pl.when(j == 0)
    def _():
        acc_ref[...] = r_ref[...]

    x = x_ref[...]
    act = (_silu(_dot(x, wg_ref[...])) * _dot(x, wu_ref[...])).astype(BF16)
    acc_ref[...] += _dot(act, wd_ref[...])

    @pl.when(j == pl.num_programs(1) - 1)
    def _():
        h = acc_ref[...]
        h_ref[...] = h
        hn_ref[...] = _rms(h, g_ref[...]).astype(hn_ref.dtype)


def ffn_res_norm(x, resid, wg, wu, wd, gain, tm=512, n_ff_tiles=2):
    t, d = x.shape
    ff = wg.shape[1]
    tf = ff // n_ff_tiles
    return pl.pallas_call(
        _ffn_kernel,
        grid=(t // tm, n_ff_tiles),
        in_specs=[pl.BlockSpec((tm, d), lambda i, j: (i, 0)),
                  pl.BlockSpec((tm, d), lambda i, j: (i, 0)),
                  pl.BlockSpec((d, tf), lambda i, j: (0, j)),
                  pl.BlockSpec((d, tf), lambda i, j: (0, j)),
                  pl.BlockSpec((tf, d), lambda i, j: (j, 0)),
                  pl.BlockSpec((1, d), lambda i, j: (0, 0))],
        out_specs=[pl.BlockSpec((tm, d), lambda i, j: (i, 0)),
                   pl.BlockSpec((tm, d), lambda i, j: (i, 0))],
        out_shape=[jax.ShapeDtypeStruct((t, d), F32),
                   jax.ShapeDtypeStruct((t, d), BF16)],
        scratch_shapes=[pltpu.VMEM((tm, d), F32)],
        compiler_params=_cparams("parallel", "arbitrary"),
        name="ffn_res_norm",
    )(x, resid, wg, wu, wd, gain.reshape(1, d))


def _mixer_kernel(p_ref, s_ref, cw_ref, gp_ref, rn_ref, gn_ref, o_ref,
                  ext_ref, ret_state, gdn_state, lvl_ref, dmask_ref, rdec_ref):
    c = SEQ_CHUNK
    d = HEAD_DIM
    step = pl.program_id(1)
    row = lax.broadcasted_iota(jnp.int32, (c, c), 0)
    col = lax.broadcasted_iota(jnp.int32, (c, c), 1)
    incl = row >= col
    strict = row > col

    @pl.when(step == 0)
    def _():
        ret_state[...] = jnp.zeros_like(ret_state)
        gdn_state[...] = jnp.zeros_like(gdn_state)
        ext_ref[0:8, :] = jnp.zeros((8, ext_ref.shape[1]), F32)
        for k in range(8):
            rb = row >> k
            cb = col >> k
            lvl_ref[k] = jnp.where((rb - cb == 1) & ((cb & 1) == 0), 1.0, 0.0).astype(BF16)
        diff = (row - col).astype(F32)
        pos = lax.broadcasted_iota(jnp.int32, (c, d), 0).astype(F32)
        for h in range(N_RET_HEADS):
            lg = math.log1p(-2.0 ** (-5.0 - h))
            dmask_ref[h] = jnp.where(incl, jnp.exp(lg * jnp.maximum(diff, 0.0)), 0.0) * d ** -0.5
            rdec_ref[h] = jnp.exp(lg * (pos + 1.0))
            rdec_ref[N_RET_HEADS + h] = jnp.exp(lg * (c - 1.0 - pos)) * d ** -0.5

    ret_old = [ret_state[h] for h in range(N_RET_HEADS)]
    gdn_old = [gdn_state[h] for h in range(N_GDN_HEADS)]
    ret_new = []
    gdn_new = []

    rheads = range(N_RET_HEADS)
    rq = [p_ref[:, h * d:(h + 1) * d] for h in rheads]
    rk = [p_ref[:, D_RET + h * d:D_RET + (h + 1) * d] for h in rheads]
    rvb = [p_ref[:, 2 * D_RET + h * d:2 * D_RET + (h + 1) * d].astype(BF16) for h in rheads]
    scores = [(_dot_nt(rq[h].astype(BF16), rk[h].astype(BF16)) * dmask_ref[h]).astype(BF16)
              for h in rheads]
    ro = [_dot(scores[h], rvb[h]) + _dot((rq[h] * rdec_ref[h]).astype(BF16), ret_old[h].astype(BF16))
          for h in rheads]
    for h in rheads:
        k_dec = (rk[h] * rdec_ref[N_RET_HEADS + h]).astype(BF16)
        ret_new.append(ret_old[h] * math.exp(math.log1p(-2.0 ** (-5.0 - h)) * c) + _dot_tn(k_dec, rvb[h]))
    for h in rheads:
        gate = p_ref[:, 3 * D_RET + h * d:3 * D_RET + (h + 1) * d]
        out = _rms(ro[h], rn_ref[h:h + 1, :]) * _silu(gate)
        o_ref[:, h * d:(h + 1) * d] = out.astype(o_ref.dtype)

    base = 4 * D_RET
    ext_ref[8:8 + c, :] = p_ref[:, base:base + 3 * D_GDN]

    def conv_silu(lo):
        acc = cw_ref[0:1, lo:lo + d] * ext_ref[5:5 + c, lo:lo + d]
        for t in range(1, CONV_WIDTH):
            acc += cw_ref[t:t + 1, lo:lo + d] * ext_ref[5 + t:5 + t + c, lo:lo + d]
        return _silu(acc)

    def l2n(x):
        return x * lax.rsqrt(jnp.sum(x * x, axis=-1, keepdims=True) + NORM_EPS)

    small = s_ref[...]
    beta_all = _sigmoid(small)
    a_all = small + gp_ref[0:1, :]
    softplus = jnp.maximum(a_all, 0.0) + jnp.log1p(jnp.exp(-jnp.abs(a_all)))
    g_all = -jnp.exp(gp_ref[1:2, :]) * softplus
    tri = jnp.where(incl, 1.0, 0.0).astype(BF16)
    g_hi = g_all.astype(BF16)
    g_r1 = g_all - g_hi.astype(F32)
    g_mid = g_r1.astype(BF16)
    g_lo = (g_r1 - g_mid.astype(F32)).astype(BF16)
    gcum_all = _dot(tri, g_hi) + _dot(tri, g_mid) + _dot(tri, g_lo)
    gcum_rows = gcum_all.T
    g_last = gcum_all[c - 1:c, :]
    eg_all = jnp.exp(gcum_all)
    kdec_all = jnp.exp(g_last - gcum_all)
    sdec = jnp.exp(g_last)
    eye_b = jnp.where(row == col, 1.0, 0.0).astype(BF16)
    heads = range(N_GDN_HEADS)
    glane = [N_GDN_HEADS + h for h in heads]
    q = [l2n(conv_silu(h * d)) * d ** -0.5 for h in heads]
    k = [l2n(conv_silu(D_GDN + h * d)) for h in heads]
    v = [conv_silu(2 * D_GDN + h * d) for h in heads]
    beta = [beta_all[:, h:h + 1] for h in heads]
    e_g = [eg_all[:, gl:gl + 1] for gl in glane]
    decay = []
    for gl in glane:
        gdiff = gcum_all[:, gl:gl + 1] - gcum_rows[gl:gl + 1, :]
        decay.append(jnp.where(incl, jnp.exp(jnp.where(incl, gdiff, 0.0)), 0.0))
    kb = [k[h].astype(BF16) for h in heads]
    k_beta = [k[h] * beta[h] for h in heads]
    lower = [jnp.where(strict, _dot_nt(k_beta[h].astype(BF16), kb[h]) * decay[h], 0.0).astype(BF16)
             for h in heads]
    attn = [jnp.where(incl, _dot_nt(q[h].astype(BF16), kb[h]) * decay[h], 0.0).astype(BF16)
            for h in heads]
    t_mat = [eye_b - lower[h] * lvl_ref[0] for h in heads]
    for lv in range(1, 8):
        inner = [_dot(lower[h] * lvl_ref[lv], t_mat[h]).astype(BF16) for h in heads]
        t_mat = [t_mat[h] - _dot(t_mat[h], inner[h]).astype(BF16) for h in heads]
    rhs = [jnp.concatenate([v[h] * beta[h], k_beta[h] * e_g[h]], axis=1).astype(BF16) for h in heads]
    uw = [_dot(t_mat[h], rhs[h]) for h in heads]
    sb = [gdn_old[h].astype(BF16) for h in heads]
    vnb = [(uw[h][:, :d] - _dot(uw[h][:, d:].astype(BF16), sb[h])).astype(BF16) for h in heads]
    o = [_dot((q[h] * e_g[h]).astype(BF16), sb[h]) + _dot(attn[h], vnb[h]) for h in heads]
    for h in heads:
        gl = glane[h]
        k_dec = (k[h] * kdec_all[:, gl:gl + 1]).astype(BF16)
        gdn_new.append(gdn_old[h] * sdec[:, gl:gl + 1] + _dot_tn(k_dec, vnb[h]))
    for h in heads:
        gate = p_ref[:, base + 3 * D_GDN + h * d:base + 3 * D_GDN + (h + 1) * d]
        out = _rms(o[h], gn_ref[h:h + 1, :]) * _silu(gate)
        o_ref[:, D_RET + h * d:D_RET + (h + 1) * d] = out.astype(o_ref.dtype)

    ext_ref[0:8, :] = ext_ref[c:c + 8, :]
    for h in range(N_RET_HEADS):
        ret_state[h] = ret_new[h]
    for h in range(N_GDN_HEADS):
        gdn_state[h] = gdn_new[h]


def ret_gdn_mixer(proj, small, conv_w, a_log, dt_bias, ret_norm, gdn_norm):
    b, l, width = proj.shape
    c = SEQ_CHUNK
    lanes = slice(N_GDN_HEADS, 2 * N_GDN_HEADS)
    gdn_params = jnp.zeros((8, LANES), F32).at[0, lanes].set(dt_bias).at[1, lanes].set(a_log)
    return pl.pallas_call(
        _mixer_kernel,
        grid=(b, l // c),
        in_specs=[pl.BlockSpec((None, c, width), lambda i, j: (i, j, 0)),
                  pl.BlockSpec((None, c, LANES), lambda i, j: (i, j, 0)),
                  pl.BlockSpec((CONV_WIDTH, 3 * D_GDN), lambda i, j: (0, 0)),
                  pl.BlockSpec((8, LANES), lambda i, j: (0, 0)),
                  pl.BlockSpec((N_RET_HEADS, HEAD_DIM), lambda i, j: (0, 0)),
                  pl.BlockSpec((N_GDN_HEADS, HEAD_DIM), lambda i, j: (0, 0))],
        out_specs=pl.BlockSpec((None, c, D_RET + D_GDN), lambda i, j: (i, j, 0)),
        out_shape=jax.ShapeDtypeStruct((b, l, D_RET + D_GDN), BF16),
        scratch_shapes=[pltpu.VMEM((c + 8, 3 * D_GDN), F32),
                        pltpu.VMEM((N_RET_HEADS, HEAD_DIM, HEAD_DIM), F32),
                        pltpu.VMEM((N_GDN_HEADS, HEAD_DIM, HEAD_DIM), F32),
                        pltpu.VMEM((8, c, c), BF16),
                        pltpu.VMEM((N_RET_HEADS, c, c), F32),
                        pltpu.VMEM((2 * N_RET_HEADS, c, HEAD_DIM), F32)],
        compiler_params=_cparams("arbitrary", "arbitrary"),
        name="ret_gdn_mixer",
    )(proj, small, conv_w, gdn_params, ret_norm, gdn_norm)


def _moba_kernel(slope_ref, q_ref, k_ref, v_ref, o_ref,
                 kmean_ref, vt_ref, bias_ref, sel_ref, sa_ref, sb_ref):
    blk = MOBA_BLOCK
    grp = MOBA_GROUP
    d = HEAD_DIM
    n_blk = kmean_ref.shape[0]
    h = pl.program_id(1)
    i = pl.program_id(2)
    slope2 = slope_ref[h] * LOG2_E
    key_pos = lax.broadcasted_iota(jnp.int32, (blk, blk), 0)
    qry_pos = lax.broadcasted_iota(jnp.int32, (blk, blk), 1)

    @pl.when(i == 0)
    def _():
        ones_row = jnp.where(lax.broadcasted_iota(jnp.int32, (VT_PAD, blk), 0) == 0,
                             1.0, 0.0).astype(BF16)

        def setup(j, carry):
            lo = pl.multiple_of(j * blk, blk)
            kmean_ref[pl.ds(j, 1), :] = jnp.mean(k_ref[pl.ds(lo, blk), :].astype(F32),
                                                 axis=0, keepdims=True)
            vt_ref[j, 0:d, :] = v_ref[pl.ds(lo, blk), :].astype(F32).T.astype(BF16)
            vt_ref[j, d:d + VT_PAD, :] = ones_row
            return carry
        lax.fori_loop(0, n_blk, setup, 0)
        bias_ref[...] = -slope2 * (qry_pos - key_pos).astype(F32)

    q_t = q_ref[...].astype(F32).T
    q_ts = (q_t * (d ** -0.5 * LOG2_E)).astype(BF16)

    blk_id = lax.broadcasted_iota(jnp.int32, (n_blk, blk), 0)
    gate = lax.dot_general(kmean_ref[...], q_t, (((1,), (0,)), ((), ())),
                           precision=lax.Precision.HIGHEST, preferred_element_type=F32)
    gate = jnp.where(blk_id < i, gate, NEG_INF)
    sel = jnp.zeros((n_blk, blk), F32)
    for _ in range(min(MOBA_TOPK, n_blk)):
        best = jnp.max(gate, axis=0, keepdims=True)
        first = jnp.min(jnp.where(gate == best, blk_id, n_blk), axis=0, keepdims=True)
        pick = blk_id == first
        sel = jnp.where(pick & (blk_id < i), 1.0, sel)
        gate = jnp.where(pick, -jnp.inf, gate)
    sel_ref[...] = sel

    start = pl.multiple_of(i * blk, blk)
    s = _dot(k_ref[pl.ds(start, blk), :], q_ts) + bias_ref[...]
    s = jnp.where(qry_pos >= key_pos, s, NEG_INF)
    m0 = jnp.max(s, axis=0, keepdims=True)
    acc0 = _dot(vt_ref[i], jnp.exp2(s - m0).astype(BF16))

    last_group = n_blk // grp - 1

    def score(g, buf):
        g = jnp.minimum(g, last_group)
        peaks = []
        for w in range(grp):
            lo = pl.multiple_of((g * grp + w) * blk, blk)
            s_w = _dot(k_ref[pl.ds(lo, blk), :], q_ts) + bias_ref[...]
            buf[w] = s_w
            peaks.append(jnp.max(s_w, axis=0, keepdims=True))
        return tuple(peaks)

    def absorb(g, buf, peaks, m, acc):
        m_new = m
        shifts = []
        for w in range(grp):
            j = g * grp + w
            chosen = (sel_ref[pl.ds(jnp.minimum(j, n_blk - 1), 1), :] > 0.0) & (j < i)
            offset = slope2 * ((j - i) * blk).astype(F32)
            m_new = jnp.maximum(m_new, jnp.where(chosen, peaks[w] + offset, -jnp.inf))
            shifts.append((chosen, offset))
        acc = jnp.exp2(m - m_new) * acc
        for w in range(grp):
            chosen, offset = shifts[w]
            p = jnp.exp2(buf[w] - jnp.where(chosen, m_new - offset, -NEG_INF))
            acc = acc + _dot(vt_ref[jnp.minimum(g * grp + w, n_blk - 1)], p.astype(BF16))
        return m_new, acc

    def body(t, carry):
        m, acc, peaks = carry
        g = MOBA_STAGES * t
        for u in range(0, MOBA_STAGES, 2):
            peaks_b = score(g + u + 1, sb_ref)
            m, acc = absorb(g + u, sa_ref, peaks, m, acc)
            peaks = score(g + u + 2, sa_ref)
            m, acc = absorb(g + u + 1, sb_ref, peaks_b, m, acc)
        return m, acc, peaks

    n_groups = (i + grp - 1) // grp
    _, acc, _ = lax.fori_loop(0, (n_groups + MOBA_STAGES - 1) // MOBA_STAGES, body,
                              (m0, acc0, score(0, sa_ref)))
    o_ref[...] = (acc[0:d, :] / acc[d:d + 1, :]).T.astype(o_ref.dtype)


def moba_attention(qkv, slopes, n_heads):
    b, l, _ = qkv.shape
    d = HEAD_DIM
    blk = MOBA_BLOCK
    n_blk = l // blk
    assert l % blk == 0 and n_blk % MOBA_GROUP == 0
    return pl.pallas_call(
        _moba_kernel,
        grid=(b, n_heads, n_blk),
        in_specs=[pl.BlockSpec(memory_space=pltpu.SMEM),
                  pl.BlockSpec((None, blk, d), lambda bi, h, i: (bi, i, h)),
                  pl.BlockSpec((None, l, d), lambda bi, h, i: (bi, 0, n_heads + h)),
                  pl.BlockSpec((None, l, d), lambda bi, h, i: (bi, 0, 2 * n_heads + h))],
        out_specs=pl.BlockSpec((None, blk, d), lambda bi, h, i: (bi, i, h)),
        out_shape=jax.ShapeDtypeStruct((b, l, n_heads * d), BF16),
        scratch_shapes=[pltpu.VMEM((n_blk, d), F32),
                        pltpu.VMEM((n_blk, d + VT_PAD, blk), BF16),
                        pltpu.VMEM((blk, blk), F32),
                        pltpu.VMEM((n_blk, blk), F32),
                        pltpu.VMEM((MOBA_GROUP, blk, blk), F32),
                        pltpu.VMEM((MOBA_GROUP, blk, blk), F32)],
        compiler_params=_cparams("arbitrary", "arbitrary", "arbitrary"),
        name="moba_attention",
    )(slopes, qkv, qkv, qkv)


def _router_kernel(a_ref, w_ref, r_ref, g_ref, wr_ref, h_ref, hn_ref, meta_ref, gw_ref,
                   cnt_ref, carry_ref):
    tm = a_ref.shape[0]

    @pl.when(pl.program_id(0) == 0)
    def _():
        carry_ref[...] = jnp.zeros_like(carry_ref)

    h = r_ref[...] + _dot(a_ref[...], w_ref[...])
    h_ref[...] = h
    hn = _rms(h, g_ref[...])
    hn_ref[...] = hn
    logits = lax.dot_general(hn, wr_ref[...], (((1,), (0,)), ((), ())),
                             precision=lax.Precision.HIGHEST, preferred_element_type=F32)
    col = lax.broadcasted_iota(jnp.int32, (tm, LANES), 1)
    logits = jnp.where(col < N_EXPERTS, logits, -jnp.inf)
    m1 = jnp.max(logits, axis=1, keepdims=True)
    i1 = jnp.min(jnp.where(logits == m1, col, LANES), axis=1, keepdims=True)
    rest = jnp.where(col == i1, -jnp.inf, logits)
    m2 = jnp.max(rest, axis=1, keepdims=True)
    i2 = jnp.min(jnp.where(rest == m2, col, LANES), axis=1, keepdims=True)
    e = jnp.exp(m2 - m1)
    g1 = 1.0 / (1.0 + e)
    g2 = e / (1.0 + e)
    pick1 = col == i1
    pick2 = col == i2
    onehot = jnp.where(pick1 | pick2, 1.0, 0.0)
    tri = jnp.where(lax.broadcasted_iota(jnp.int32, (tm, tm), 0)
                    > lax.broadcasted_iota(jnp.int32, (tm, tm), 1), 1.0, 0.0).astype(BF16)
    rank = _dot(tri, onehot.astype(BF16)) + carry_ref[0:1, :]
    carry = carry_ref[0:1, :] + jnp.sum(onehot, axis=0, keepdims=True)
    carry_ref[...] = jnp.broadcast_to(carry, carry_ref.shape)
    cnt_ref[...] = jnp.broadcast_to(carry, cnt_ref.shape).astype(jnp.int32)
    r1 = jnp.sum(jnp.where(pick1, rank, 0.0), axis=1, keepdims=True).astype(jnp.int32)
    r2 = jnp.sum(jnp.where(pick2, rank, 0.0), axis=1, keepdims=True).astype(jnp.int32)
    meta_ref[...] = jnp.where(col == 0, i1, jnp.where(col == 1, i2, jnp.where(col == 2, r1, r2)))
    gw_ref[...] = jnp.where(col == 0, g1, g2)


def attn_out_router(a, w, resid, gain, w_router, tm=512):
    t, k = a.shape
    d = w.shape[1]
    wr = jnp.zeros((d, LANES), F32).at[:, :N_EXPERTS].set(w_router)
    return pl.pallas_call(
        _router_kernel,
        grid=(t // tm,),
        in_specs=[pl.BlockSpec((tm, k), lambda i: (i, 0)),
                  pl.BlockSpec((k, d), lambda i: (0, 0)),
                  pl.BlockSpec((tm, d), lambda i: (i, 0)),
                  pl.BlockSpec((1, d), lambda i: (0, 0)),
                  pl.BlockSpec((d, LANES), lambda i: (0, 0))],
        out_specs=[pl.BlockSpec((tm, d), lambda i: (i, 0)),
                   pl.BlockSpec((tm, d), lambda i: (i, 0)),
                   pl.BlockSpec((tm, LANES), lambda i: (i, 0)),
                   pl.BlockSpec((tm, LANES), lambda i: (i, 0)),
                   pl.BlockSpec((8, LANES), lambda i: (0, 0))],
        out_shape=[jax.ShapeDtypeStruct((t, d), F32),
                   jax.ShapeDtypeStruct((t, d), F32),
                   jax.ShapeDtypeStruct((t, LANES), jnp.int32),
                   jax.ShapeDtypeStruct((t, LANES), F32),
                   jax.ShapeDtypeStruct((8, LANES), jnp.int32)],
        scratch_shapes=[pltpu.VMEM((8, LANES), F32)],
        compiler_params=_cparams("arbitrary"),
        name="attn_out_router",
    )(a, w, resid, gain.reshape(1, d), wr)


def _dispatch_kernel(dest_ref, x_ref, buf_in_ref, buf_ref, sem):
    del buf_in_ref
    tb = x_ref.shape[0]

    def copy(r, slot):
        return pltpu.make_async_copy(x_ref.at[pl.ds(r, 1)], buf_ref.at[pl.ds(slot, 1)], sem)

    def issue(r, carry):
        copy(r, dest_ref[0, r]).start(priority=0)
        copy(r, dest_ref[0, tb + r]).start(priority=1)
        return carry

    def drain(r, carry):
        copy(r, dest_ref[0, r]).wait()
        copy(r, dest_ref[0, tb + r]).wait()
        return carry

    lax.fori_loop(0, tb, issue, 0, unroll=4)
    lax.fori_loop(0, tb, drain, 0, unroll=4)


def moe_dispatch(x, dest, n_slots, tb=512):
    t, d = x.shape
    nb = t // tb
    dest_blocks = jnp.concatenate([dest[0].reshape(nb, 1, tb), dest[1].reshape(nb, 1, tb)], axis=2)
    return pl.pallas_call(
        _dispatch_kernel,
        grid=(nb,),
        in_specs=[pl.BlockSpec((None, 1, 2 * tb), lambda i: (i, 0, 0), memory_space=pltpu.SMEM),
                  pl.BlockSpec((tb, d), lambda i: (i, 0)),
                  pl.BlockSpec(memory_space=pl.ANY)],
        out_specs=pl.BlockSpec(memory_space=pl.ANY),
        out_shape=jax.ShapeDtypeStruct((n_slots, d), x.dtype),
        scratch_shapes=[pltpu.SemaphoreType.DMA],
        input_output_aliases={2: 0},
        compiler_params=_cparams("arbitrary"),
        name="moe_dispatch",
    )(dest_blocks, x, jnp.zeros((n_slots, d), x.dtype))


def _expert_kernel(be_ref, x_ref, wg_ref, wu_ref, wd_ref, y_ref, xb_ref, acc_ref):
    del be_ref
    j = pl.program_id(1)

    @pl.when(j == 0)
    def _():
        xb_ref[...] = x_ref[...].astype(BF16)

    x = xb_ref[...]
    act = (_silu(_dot(x, wg_ref[...])) * _dot(x, wu_ref[...])).astype(BF16)
    part = _dot(act, wd_ref[...])

    @pl.when(j == 0)
    def _():
        acc_ref[...] = part

    @pl.when(j > 0)
    def _():
        acc_ref[...] += part

    @pl.when(j == pl.num_programs(1) - 1)
    def _():
        y_ref[...] = acc_ref[...]


def moe_experts(buf, block_expert, wg, wu, wd, tm=EXPERT_ROW_BLOCK, n_ff_tiles=2):
    n_slots, d = buf.shape
    ff = wg.shape[2]
    tf = ff // n_ff_tiles
    grid_spec = pltpu.PrefetchScalarGridSpec(
        num_scalar_prefetch=1,
        grid=(n_slots // tm, n_ff_tiles),
        in_specs=[pl.BlockSpec((tm, d), lambda i, j, be: (i, 0)),
                  pl.BlockSpec((None, d, tf), lambda i, j, be: (be[i], 0, j)),
                  pl.BlockSpec((None, d, tf), lambda i, j, be: (be[i], 0, j)),
                  pl.BlockSpec((None, tf, d), lambda i, j, be: (be[i], j, 0))],
        out_specs=pl.BlockSpec((tm, d), lambda i, j, be: (i, 0)),
        scratch_shapes=[pltpu.VMEM((tm, d), BF16), pltpu.VMEM((tm, d), F32)],
    )
    return pl.pallas_call(
        _expert_kernel,
        grid_spec=grid_spec,
        out_shape=jax.ShapeDtypeStruct((n_slots, d), F32),
        compiler_params=_cparams("parallel", "arbitrary"),
        name="moe_experts",
    )(block_expert, buf, wg, wu, wd)


def _combine_kernel(dest_ref, next_ref, y_ref, h_ref, gw_ref, g_ref, o_ref, rows_ref, sems):
    tb = h_ref.shape[0]
    s = pl.program_id(0)
    slot = s % 2

    def copy(idx_ref, k, r, into):
        return pltpu.make_async_copy(y_ref.at[pl.ds(idx_ref[0, k * tb + r], 1)],
                                     rows_ref.at[into, k, pl.ds(r, 1)], sems.at[into])

    def start_all(idx_ref, into):
        def issue(r, carry):
            copy(idx_ref, 0, r, into).start(priority=0)
            copy(idx_ref, 1, r, into).start(priority=1)
            return carry
        lax.fori_loop(0, tb, issue, 0, unroll=4)

    @pl.when(s == 0)
    def _():
        start_all(dest_ref, 0)

    @pl.when(s + 1 < pl.num_programs(0))
    def _():
        start_all(next_ref, 1 - slot)

    def drain(r, carry):
        copy(dest_ref, 0, r, slot).wait()
        copy(dest_ref, 1, r, slot).wait()
        return carry
    lax.fori_loop(0, tb, drain, 0, unroll=4)

    gw = gw_ref[...]
    h = h_ref[...] + gw[:, 0:1] * rows_ref[slot, 0] + gw[:, 1:2] * rows_ref[slot, 1]
    o_ref[...] = _rms(h, g_ref[...])


def moe_combine_norm(y, dest, h, gw, gain, tb=256):
    t, d = h.shape
    nb = t // tb
    dest_blocks = jnp.concatenate([dest[0].reshape(nb, 1, tb), dest[1].reshape(nb, 1, tb)], axis=2)
    idx_spec = functools.partial(pl.BlockSpec, (None, 1, TOP_K * tb), memory_space=pltpu.SMEM)
    return pl.pallas_call(
        _combine_kernel,
        grid=(nb,),
        in_specs=[idx_spec(lambda i: (i, 0, 0)),
                  idx_spec(lambda i: (jnp.minimum(i + 1, nb - 1), 0, 0)),
                  pl.BlockSpec(memory_space=pl.ANY),
                  pl.BlockSpec((tb, d), lambda i: (i, 0)),
                  pl.BlockSpec((tb, LANES), lambda i: (i, 0)),
                  pl.BlockSpec((1, d), lambda i: (0, 0))],
        out_specs=pl.BlockSpec((tb, d), lambda i: (i, 0)),
        out_shape=jax.ShapeDtypeStruct((t, d), F32),
        scratch_shapes=[pltpu.VMEM((2, TOP_K, tb, d), F32), pltpu.SemaphoreType.DMA((2,))],
        compiler_params=_cparams("arbitrary"),
        name="moe_combine_norm",
    )(dest_blocks, dest_blocks, y, h, gw, gain.reshape(1, d))


def kernel(x, mix_norm, ffn_norm, even_w_in, even_conv_w, even_a_log, even_dt_bias, even_ret_norm,
           even_gdn_norm, even_w_out, odd_w_qkv, odd_w_out, ffn_w_gate, ffn_w_up, ffn_w_down,
           moe_router, moe_w_gate, moe_w_up, moe_w_down, final_norm):
    b, l, d = x.shape
    t = b * l
    x2 = x.reshape(t, d)

    n_main = 4 * D_RET + 4 * D_GDN
    w_in = even_w_in[0]
    w_main = w_in[:, :n_main].astype(BF16)
    w_small = jnp.zeros((d, LANES), BF16).at[:, :2 * N_GDN_HEADS].set(w_in[:, n_main:].astype(BF16))
    hn = rmsnorm(x2, mix_norm[0], BF16)
    proj = matmul(hn, w_main, F32)
    small = matmul(hn, w_small, F32)
    mixed = ret_gdn_mixer(proj.reshape(b, l, n_main), small.reshape(b, l, LANES),
                          even_conv_w[0], even_a_log[0], even_dt_bias[0],
                          even_ret_norm[0], even_gdn_norm[0])
    h1, hn1 = matmul_res_norm(mixed.reshape(t, d), even_w_out[0].astype(BF16), x2, ffn_norm[0])
    h2, hn2 = ffn_res_norm(hn1, h1, ffn_w_gate[0].astype(BF16), ffn_w_up[0].astype(BF16),
                           ffn_w_down[0].astype(BF16), mix_norm[1])

    n_heads = d // HEAD_DIM
    qkv = matmul(hn2, odd_w_qkv[0].astype(BF16), BF16)
    slopes = jnp.exp2(-8.0 * jnp.arange(1, n_heads + 1, dtype=F32) / n_heads)
    attn = moba_attention(qkv.reshape(b, l, 3 * d), slopes, n_heads)
    h3, hn3, meta, gw, counts = attn_out_router(attn.reshape(t, d), odd_w_out[0].astype(BF16), h2,
                                                ffn_norm[1], moe_router[0])

    rb = EXPERT_ROW_BLOCK
    n_assign = t * TOP_K
    n_blocks = -(-n_assign // rb) + N_EXPERTS
    cnt = counts[0, :N_EXPERTS]
    padded = (cnt + rb - 1) // rb * rb
    padded_end = jnp.cumsum(padded)
    padded_start = padded_end - padded
    dest = jnp.stack([padded_start[meta[:, 0]] + meta[:, 2],
                      padded_start[meta[:, 1]] + meta[:, 3]]).astype(jnp.int32)
    block_start = jnp.arange(n_blocks, dtype=jnp.int32) * rb
    block_expert = jnp.minimum(
        jnp.sum((padded_end[None, :] <= block_start[:, None]).astype(jnp.int32), axis=1),
        N_EXPERTS - 1)

    buf = moe_dispatch(hn3, dest, n_blocks * rb)
    y = moe_experts(buf, block_expert, moe_w_gate[0].astype(BF16), moe_w_up[0].astype(BF16),
                    moe_w_down[0].astype(BF16))
    out = moe_combine_norm(y, dest, h3, gw, final_norm)
    return out.reshape(b, l, d)
```
